```python
import math
import jax, jax.numpy as jnp
from jax import lax
import numpy as np

D_MODEL = 1024
BATCH = 32
SEQ = 2048
DEPTH = 1

MOBA_HEADS = 8
MOBA_HEAD_DIM = 64
MOBA_WIDTH = MOBA_HEADS * MOBA_HEAD_DIM
MOBA_BLOCK = 256
MOBA_TOPK = 3
MOBA_QCHUNK = 16
HGRN_HEADS = 4
HGRN_EXPAND = 128
HGRN_HEAD_DIM = 128
HGRN_WIDTH = HGRN_HEADS * HGRN_EXPAND
HGRN_CHUNK = 64
IN_SIZES = (MOBA_WIDTH, MOBA_WIDTH, MOBA_WIDTH,
            HGRN_WIDTH, HGRN_WIDTH, HGRN_WIDTH, HGRN_WIDTH,
            D_MODEL, D_MODEL)
IN_COLS = sum(IN_SIZES)
PEER_HEADS = 8
PEER_NKEYS = 128
PEER_EXPERTS = PEER_NKEYS * PEER_NKEYS
PEER_KEY_DIM = 128
PEER_HALF = PEER_KEY_DIM // 2
PEER_TOPK = 16
PEER_TOKEN_CHUNK = 512
DEEPNORM_ALPHA = (2.0 * DEPTH) ** 0.25
DEEPNORM_BETA = (8.0 * DEPTH) ** -0.25
LN_EPS = 1e-5
RMS_EPS = 1e-6

kernel_name = "hybrid_moba_hgrn2_peer_deepnorm_adaln"

F32 = jnp.float32


def layer_norm(x, g, b):
    xf = x.astype(F32)
    mu = jnp.mean(xf, -1, keepdims=True)
    var = jnp.mean(jnp.square(xf - mu), -1, keepdims=True)
    y = (xf - mu) * lax.rsqrt(var + LN_EPS)
    return (y * g.astype(F32) + b.astype(F32)).astype(x.dtype)


def alibi_slopes(n_heads):
    return jnp.exp2(-8.0 * jnp.arange(1, n_heads + 1, dtype=F32) / n_heads)


def moba_attention(q, k, v):
    B, S, H, dh = q.shape
    BS = MOBA_BLOCK
    nb = -(-S // BS)
    s_pad = nb * BS
    pad = ((0, 0), (0, s_pad - S), (0, 0), (0, 0))
    q = jnp.pad(q, pad).transpose(0, 2, 1, 3) * (dh ** -0.5)
    k = jnp.pad(k, pad).transpose(0, 2, 1, 3)
    v = jnp.pad(v, pad).transpose(0, 2, 1, 3)
    kb = k.reshape(B, H, nb, BS, dh)
    vb = v.reshape(B, H, nb, BS, dh)
    kmean = jnp.mean(kb.astype(F32), axis=3)
    gate = jnp.einsum('bhsd,bhnd->bhsn', q.astype(F32), kmean)
    qblk = jnp.arange(s_pad) // BS
    past = jnp.arange(nb)[None, :] < qblk[:, None]
    gate = jnp.where(past, gate, -jnp.inf)
    topk = min(MOBA_TOPK, nb)
    gval, gidx = lax.top_k(gate, topk)
    gvalid = jnp.isfinite(gval)
    slopes = alibi_slopes(H)
    bi = jnp.arange(B)[:, None, None, None]
    hi = jnp.arange(H)[None, :, None, None]
    offs = jnp.arange(BS)
    QC = MOBA_QCHUNK
    n_q = s_pad // QC

    def chunk(ci):
        p = ci * QC
        qc = lax.dynamic_slice_in_dim(q, p, QC, axis=2)
        idx = lax.dynamic_slice_in_dim(gidx, p, QC, axis=2)
        val = lax.dynamic_slice_in_dim(gvalid, p, QC, axis=2)
        qpos = p + jnp.arange(QC)
        j = p // BS
        k_own = lax.dynamic_index_in_dim(kb, j, axis=2, keepdims=False)
        v_own = lax.dynamic_index_in_dim(vb, j, axis=2, keepdims=False)
        dist_own = (qpos[:, None] - (j * BS + offs)[None, :]).astype(F32)
        s_own = (jnp.einsum('bhqd,bhkd->bhqk', qc, k_own).astype(F32)
                 - slopes[None, :, None, None] * dist_own[None, None])
        s_own = jnp.where(dist_own[None, None] >= 0, s_own, -jnp.inf)
        k_sel = kb[bi, hi, idx]
        v_sel = vb[bi, hi, idx]
        kpos_sel = idx[..., None] * BS + offs
        dist_sel = (qpos[None, None, :, None, None] - kpos_sel).astype(F32)
        s_sel = (jnp.einsum('bhqd,bhqnkd->bhqnk', qc, k_sel).astype(F32)
                 - slopes[None, :, None, None, None] * dist_sel)
        s_sel = jnp.where(val[..., None], s_sel, -jnp.inf)
        scores = jnp.concatenate([s_own, s_sel.reshape(B, H, QC, topk * BS)], axis=-1)
        probs = jax.nn.softmax(scores, axis=-1).astype(v.dtype)
        p_own = probs[..., :BS]
        p_sel = probs[..., BS:].reshape(B, H, QC, topk, BS)
        return (jnp.einsum('bhqk,bhkd->bhqd', p_own, v_own)
                + jnp.einsum('bhqnk,bhqnkd->bhqd', p_sel, v_sel))

    out = lax.map(chunk, jnp.arange(n_q))
    out = out.transpose(1, 0, 3, 2, 4).reshape(B, s_pad, H, dh)[:, :S]
    return out.reshape(B, S, H * dh)


def hgrn2(q, f_logit, i, g, lb, norm_g):
    B, S, _ = q.shape
    H, dk, dv, C = HGRN_HEADS, HGRN_EXPAND, HGRN_HEAD_DIM, HGRN_CHUNK
    nC = S // C

    def heads(t, d):
        return t.astype(F32).reshape(B, nC, C, H, d).transpose(0, 3, 1, 2, 4)

    lbf = lb.astype(F32)
    f = lbf + (1.0 - lbf) * jax.nn.sigmoid(f_logit.astype(F32))
    qh = heads(jax.nn.silu(q.astype(F32)), dk)
    kh = heads(1.0 - f, dk)
    lfh = heads(jnp.log(f), dk)
    vh = heads(i, dv)
    b = jnp.cumsum(lfh, axis=3)
    b_last = b[:, :, :, -1:, :]
    b_ref = b[:, :, :, C // 2:C // 2 + 1, :]
    a = jnp.einsum('bhntd,bhnsd->bhnts', qh * jnp.exp(b - b_ref), kh * jnp.exp(b_ref - b))
    causal = jnp.tril(jnp.ones((C, C), dtype=bool))
    a = jnp.where(causal, a, 0.0)
    o_intra = jnp.einsum('bhnts,bhnsv->bhntv', a, vh)
    ds = jnp.einsum('bhnsd,bhnsv->nbhdv', kh * jnp.exp(b_last - b), vh)
    decay = jnp.exp(b_last[:, :, :, 0, :]).transpose(2, 0, 1, 3)

    def step(state, inp):
        d, dsc = inp
        return d[..., None] * state + dsc, state

    s0 = jnp.zeros((B, H, dk, dv), F32)
    _, s_prev = lax.scan(step, s0, (decay, ds))
    o_inter = jnp.einsum('bhntd,nbhdv->bhntv', qh * jnp.exp(b), s_prev)
    o = (o_intra + o_inter).transpose(0, 2, 3, 1, 4).reshape(B, S, H, dv)
    o = o * lax.rsqrt(jnp.mean(o * o, axis=-1, keepdims=True) + RMS_EPS)
    o = o.reshape(B, S, H * dv) * norm_g.astype(F32) * jax.nn.silu(g.astype(F32))
    return o.astype(q.dtype)


def peer(h, wq, subkeys, u_tab, v_tab):
    B, S, D = h.shape
    T = B * S
    tc = math.gcd(T, PEER_TOKEN_CHUNK)
    H, K, NK = PEER_HEADS, PEER_TOPK, PEER_NKEYS
    hf = h.reshape(T // tc, tc, D)

    def chunk(xc):
        q = (xc @ wq).astype(F32).reshape(tc, H, 2, PEER_HALF)
        s = jnp.einsum('thpd,hpnd->thpn', q, subkeys.astype(F32))
        sv, si = lax.top_k(s, K)
        cand = (sv[:, :, 0, :, None] + sv[:, :, 1, None, :]).reshape(tc, H, K * K)
        cidx = (si[:, :, 0, :, None] * NK + si[:, :, 1, None, :]).reshape(tc, H, K * K)
        best, pos = lax.top_k(cand, K)
        eidx = jnp.take_along_axis(cidx, pos, axis=-1)
        gates = jax.nn.softmax(best, axis=-1)
        u = u_tab[eidx]
        act = jax.nn.gelu(jnp.einsum('td,thkd->thk', xc, u).astype(F32), approximate=False)
        w = (gates * act).astype(xc.dtype)
        return jnp.einsum('thk,thkd->td', w, v_tab[eidx])

    return lax.map(chunk, hf).reshape(B, S, D)


def setup_inputs(seed: int = 0) -> dict:
    key = jax.random.key(seed)
    ks = jax.random.split(key, 20)
    D, L = D_MODEL, DEPTH
    n = jax.random.normal
    return {
        'x': n(ks[0], (BATCH, SEQ, D), F32),
        'c': n(ks[1], (BATCH, D), F32),
        'w_ada': n(ks[2], (L, D, 6 * D), F32) * (0.5 * D ** -0.5),
        'b_ada': n(ks[3], (L, 6 * D), F32) * 0.01,
        'w_in': n(ks[4], (L, D, IN_COLS), F32) * D ** -0.5,
        'w_branch_a': n(ks[5], (L, MOBA_WIDTH, D), F32) * (MOBA_WIDTH ** -0.5 * DEEPNORM_BETA),
        'w_branch_b': n(ks[6], (L, HGRN_WIDTH, D), F32) * (HGRN_WIDTH ** -0.5 * DEEPNORM_BETA),
        'w_out': n(ks[7], (L, D, D), F32) * (D ** -0.5 * DEEPNORM_BETA),
        'hgrn_lb': 1.0 + 0.1 * n(ks[8], (L + 1, HGRN_WIDTH), F32),
        'hgrn_norm_g': 1.0 + 0.01 * n(ks[9], (L, HGRN_WIDTH), F32),
        'ln1_g': 1.0 + 0.01 * n(ks[10], (L, D), F32),
        'ln1_b': 0.01 * n(ks[11], (L, D), F32),
        'peer_wq': n(ks[12], (L, D, PEER_HEADS * PEER_KEY_DIM), F32) * D ** -0.5,
        'peer_subkeys': n(ks[13], (L, PEER_HEADS, 2, PEER_NKEYS, PEER_HALF), F32) * PEER_HALF ** -0.5,
        'peer_u': n(ks[14], (L, PEER_EXPERTS, D), F32) * D ** -0.5,
        'peer_v': n(ks[15], (L, PEER_EXPERTS, D), F32) * DEEPNORM_BETA,
        'ln2_g': 1.0 + 0.01 * n(ks[16], (L, D), F32),
        'ln2_b': 0.01 * n(ks[17], (L, D), F32),
    }


def reference(x, c, w_ada, b_ada, w_in, w_branch_a, w_branch_b, w_out, hgrn_lb, hgrn_norm_g,
              ln1_g, ln1_b, peer_wq, peer_subkeys, peer_u, peer_v, ln2_g, ln2_b):
    B, S, D = x.shape
    lb_all = jnp.cumsum(jax.nn.softmax(hgrn_lb.astype(F32), axis=0), axis=0)
    split_at = [int(v) for v in np.cumsum(IN_SIZES)[:-1]]
    for l in range(DEPTH):
        ada = jax.nn.silu(c) @ w_ada[l] + b_ada[l]
        sh1, sc1, gt1, sh2, sc2, gt2 = jnp.split(ada[:, None, :], 6, axis=-1)
        h = x * (1.0 + sc1) + sh1
        proj = h @ w_in[l]
        qa, ka, va, hq, hf, hi, hg, ga, gb = jnp.split(proj, split_at, axis=-1)
        mh = (B, S, MOBA_HEADS, MOBA_HEAD_DIM)
        ya = moba_attention(qa.reshape(mh), ka.reshape(mh), va.reshape(mh))
        yb = hgrn2(hq, hf, hi, hg, lb_all[l], hgrn_norm_g[l])
        mixed = (jax.nn.sigmoid(ga) * (ya @ w_branch_a[l])
                 + jax.nn.sigmoid(gb) * (yb @ w_branch_b[l]))
        x = layer_norm(DEEPNORM_ALPHA * x + gt1 * (mixed @ w_out[l]), ln1_g[l], ln1_b[l])
        h2 = x * (1.0 + sc2) + sh2
        y = peer(h2, peer_wq[l], peer_subkeys[l], peer_u[l], peer_v[l])
        x = layer_norm(DEEPNORM_ALPHA * x + gt2 * y, ln2_g[l], ln2_b[l])
    return x
```

```python
import functools

import jax
import jax.numpy as jnp
from jax import lax
from jax.experimental import pallas as pl
from jax.experimental.pallas import tpu as pltpu

F32 = jnp.float32
BF16 = jnp.bfloat16

LANES = 128
VMEM_LIMIT = 56 * 1024 * 1024

MOBA_HEADS = 8
MOBA_HEAD_DIM = 64
MOBA_BLOCK = 256
MOBA_TOPK = 3
HGRN_HEADS = 4
HGRN_DIM = 128
HGRN_CHUNK = 64
PEER_HEADS = 8
PEER_NKEYS = 128
PEER_HALF = 64
PEER_TOPK = 16
LN_EPS = 1e-5
RMS_EPS = 1e-6
NEG = -1e30

NT_DIMS = (((1,), (1,)), ((), ()))


def _params(sem):
    return pltpu.CompilerParams(dimension_semantics=sem, vmem_limit_bytes=VMEM_LIMIT)


def _silu(v):
    return v * jax.nn.sigmoid(v)


def _col(row):
    return jnp.broadcast_to(row, (LANES, row.shape[1])).T


def _rep(col, n):
    return jnp.concatenate([col] * (n // LANES), axis=1)


def _ada_kernel(c_ref, w_ref, b_ref, o_ref):
    sc = _silu(c_ref[...])
    o_ref[...] = jnp.dot(sc, w_ref[...], precision=lax.Precision.HIGHEST,
                         preferred_element_type=F32) + b_ref[...]


def _ada(c, w, b):
    B, D = c.shape
    N = w.shape[1]
    tn = 1536
    return pl.pallas_call(
        _ada_kernel,
        out_shape=jax.ShapeDtypeStruct((B, N), F32),
        grid=(N // tn,),
        in_specs=[pl.BlockSpec((B, D), lambda j: (0, 0)),
                  pl.BlockSpec((D, tn), lambda j: (0, j)),
                  pl.BlockSpec((1, tn), lambda j: (0, j))],
        out_specs=pl.BlockSpec((B, tn), lambda j: (0, j)),
        compiler_params=_params(("arbitrary",)),
        name="ada",
    )(c, w, b.reshape(1, N))


def _in_proj_kernel(x_ref, sc_ref, sh_ref, ws_ref, wt_ref, os_ref, ot_ref):
    h = (x_ref[...] * (1.0 + sc_ref[...]) + sh_ref[...]).astype(BF16)
    cw = 512
    for j in range(ws_ref.shape[1] // cw):
        os_ref[:, j * cw:(j + 1) * cw] = jnp.dot(
            h, ws_ref[:, j * cw:(j + 1) * cw], preferred_element_type=F32).astype(BF16)
    for j in range(wt_ref.shape[0] // cw):
        ot_ref[j * cw:(j + 1) * cw, :] = lax.dot_general(
            wt_ref[j * cw:(j + 1) * cw, :], h, NT_DIMS, preferred_element_type=F32).astype(BF16)


def _in_proj(x2, sc, sh, w_std, w_tr, S):
    T, D = x2.shape
    ns, nt = w_std.shape[1], w_tr.shape[0]
    tm = 512
    per_b = S // tm
    return pl.pallas_call(
        _in_proj_kernel,
        out_shape=(jax.ShapeDtypeStruct((T, ns), BF16), jax.ShapeDtypeStruct((nt, T), BF16)),
        grid=(T // tm,),
        in_specs=[pl.BlockSpec((tm, D), lambda i: (i, 0)),
                  pl.BlockSpec((None, 1, D), lambda i: (i // per_b, 0, 0)),
                  pl.BlockSpec((None, 1, D), lambda i: (i // per_b, 0, 0)),
                  pl.BlockSpec((D, ns), lambda i: (0, 0)),
                  pl.BlockSpec((nt, D), lambda i: (0, 0))],
        out_specs=(pl.BlockSpec((tm, ns), lambda i: (i, 0)),
                   pl.BlockSpec((nt, tm), lambda i: (0, i))),
        compiler_params=_params(("parallel",)),
        name="in_proj",
    )(x2, sc, sh, w_std, w_tr)


def _moba_kernel(q_ref, k_ref, vt_ref, o_ref):
    S = q_ref.shape[0]
    BS, dh = MOBA_BLOCK, MOBA_HEAD_DIM
    nb = S // BS
    hp = pl.program_id(1)
    q2 = q_ref[...]
    k2 = k_ref[...]
    kmean = jnp.mean(k2.astype(F32).reshape(nb, BS, LANES), axis=1)
    lane = lax.broadcasted_iota(jnp.int32, (1, LANES), 1)
    krow = lax.broadcasted_iota(jnp.int32, (S, LANES), 0)
    pos_hi = (krow >> 4).astype(F32)
    pos_lo = (krow & 15).astype(F32)
    nidx = lax.broadcasted_iota(jnp.int32, (nb, S), 0)
    qblk = lax.broadcasted_iota(jnp.int32, (nb, S), 1) // BS
    past = nidx < qblk
    causal = (lax.broadcasted_iota(jnp.int32, (BS, BS), 0)
              <= lax.broadcasted_iota(jnp.int32, (BS, BS), 1))

    for hh in range(2):
        head = (hp * 2 + hh).astype(F32)
        slope = jnp.exp2(jnp.full((1, LANES), -8.0 / MOBA_HEADS, F32) * (head + 1.0))
        own = (lane >= hh * dh) & (lane < (hh + 1) * dh)
        f0 = (1 - hh) * dh
        kh = jnp.where(own, k2.astype(F32),
                       jnp.where(lane == f0, pos_hi, jnp.where(lane == f0 + 1, pos_lo, 0.0))
                       ).astype(BF16)
        qh = jnp.where(own, q2.astype(F32) * (dh ** -0.5),
                       jnp.where(lane == f0, 16.0 * slope, jnp.where(lane == f0 + 1, slope, 0.0))
                       ).astype(BF16)
        kmh = jnp.where(own, kmean, 0.0).astype(BF16)
        gate = lax.dot_general(kmh, qh, NT_DIMS, preferred_element_type=F32)
        gate = jnp.where(past, gate, -jnp.inf)
        rank = jnp.zeros((nb, S), F32)
        for n2 in range(nb):
            row = gate[n2:n2 + 1, :]
            beats = (row > gate) | ((row == gate) & (n2 < nidx))
            rank = rank + jnp.where(beats, 1.0, 0.0)
        sel = past & (rank < float(MOBA_TOPK))

        for j in range(nb):
            qj = qh[j * BS:(j + 1) * BS, :]
            st = lax.dot_general(kh[:(j + 1) * BS, :], qj, NT_DIMS,
                                 preferred_element_type=F32)
            blocks = []
            for n in range(j):
                keep = sel[n:n + 1, j * BS:(j + 1) * BS]
                blocks.append(jnp.where(keep, st[n * BS:(n + 1) * BS, :], NEG))
            blocks.append(jnp.where(causal, st[j * BS:(j + 1) * BS, :], NEG))
            m = blocks[0].max(axis=0, keepdims=True)
            for blk in blocks[1:]:
                m = jnp.maximum(m, blk.max(axis=0, keepdims=True))
            l = jnp.zeros((1, BS), F32)
            acc = jnp.zeros((LANES, BS), F32)
            for n, blk in enumerate(blocks):
                p = jnp.exp(blk - m)
                l = l + p.sum(axis=0, keepdims=True)
                acc = acc + jnp.dot(vt_ref[:, n * BS:(n + 1) * BS], p.astype(BF16),
                                    preferred_element_type=F32)
            o = acc[hh * dh:(hh + 1) * dh, :] * (1.0 / l)
            o_ref[hh * dh:(hh + 1) * dh, j * BS:(j + 1) * BS] = o.astype(BF16)


def _moba(proj_std, proj_t, B, S, q_blk, k_blk, v_blk):
    T = B * S
    width = MOBA_HEADS * MOBA_HEAD_DIM
    npair = width // LANES
    return pl.pallas_call(
        _moba_kernel,
        out_shape=jax.ShapeDtypeStruct((width, T), BF16),
        grid=(B, npair),
        in_specs=[pl.BlockSpec((S, LANES), lambda b, h: (b, q_blk + h)),
                  pl.BlockSpec((S, LANES), lambda b, h: (b, k_blk + h)),
                  pl.BlockSpec((LANES, S), lambda b, h: (v_blk + h, b))],
        out_specs=pl.BlockSpec((LANES, S), lambda b, h: (h, b)),
        compiler_params=_params(("parallel", "parallel")),
        name="moba",
    )(proj_std, proj_std, proj_t)


def _hgrn_kernel(q_ref, f_ref, i_ref, g_ref, lb_ref, ng_ref, o_ref, ds_scr, st_scr):
    S, d = q_ref.shape
    C = HGRN_CHUNK
    nC = S // C
    lbl = lb_ref[...]
    e = jnp.exp(lbl - lbl.max(axis=0, keepdims=True))
    lb = e[0:1, :] / e.sum(axis=0, keepdims=True)
    f = lb + (1.0 - lb) * jax.nn.sigmoid(f_ref[...].astype(F32))
    lf = jnp.log(f)
    pos = lax.broadcasted_iota(jnp.int32, (S, d), 0) & (C - 1)
    b = lf
    sh = 1
    while sh < C:
        b = b + jnp.where(pos >= sh, pltpu.roll(b, sh, axis=0), 0.0)
        sh *= 2
    b3 = b.reshape(nC, C, d)
    bref = b3[:, C // 2:C // 2 + 1, :]
    blast = b3[:, C - 1:C, :]
    q3 = _silu(q_ref[...].astype(F32)).reshape(nC, C, d)
    k3 = (1.0 - f).reshape(nC, C, d)
    v3 = i_ref[...].reshape(nC, C, d)
    qe = (q3 * jnp.exp(b3 - bref)).astype(BF16)
    ke = (k3 * jnp.exp(bref - b3)).astype(BF16)
    a = jnp.einsum('ctd,csd->cts', qe, ke, preferred_element_type=F32)
    tril = (lax.broadcasted_iota(jnp.int32, (C, C), 0) >= lax.broadcasted_iota(jnp.int32, (C, C), 1))
    a = jnp.where(tril[None], a, 0.0).astype(BF16)
    o_intra = jnp.einsum('cts,csv->ctv', a, v3, preferred_element_type=F32)
    kd = (k3 * jnp.exp(blast - b3)).astype(BF16)
    v3t = jnp.swapaxes(v3.astype(F32), 1, 2).astype(BF16)
    ds_scr[...] = jnp.einsum('cvs,csd->cvd', v3t, kd, preferred_element_type=F32)
    decay = jnp.exp(blast)

    st = jnp.zeros((d, d), F32)
    for c in range(nC):
        st_scr[c] = st
        st = st * decay[c] + ds_scr[c]
    qb = (q3 * jnp.exp(b3)).astype(BF16)
    o_inter = jnp.einsum('ctd,cvd->ctv', qb, st_scr[...].astype(BF16), preferred_element_type=F32)
    o = o_intra + o_inter
    o = o * lax.rsqrt(jnp.mean(o * o, axis=-1, keepdims=True) + RMS_EPS)
    o = o.reshape(S, d) * ng_ref[...] * _silu(g_ref[...].astype(F32))
    o_ref[...] = o.astype(BF16)


def _hgrn(proj_std, lb, norm_g, B, S, q_blk):
    T = B * S
    H, d = HGRN_HEADS, HGRN_DIM
    nC = S // HGRN_CHUNK

    def col(off):
        return pl.BlockSpec((S, d), lambda b, h: (b, q_blk + off * H + h))

    return pl.pallas_call(
        _hgrn_kernel,
        out_shape=jax.ShapeDtypeStruct((T, H * d), BF16),
        grid=(B, H),
        in_specs=[col(0), col(1), col(2), col(3),
                  pl.BlockSpec((lb.shape[0], d), lambda b, h: (0, h)),
                  pl.BlockSpec((1, d), lambda b, h: (0, h))],
        out_specs=pl.BlockSpec((S, d), lambda b, h: (b, h)),
        scratch_shapes=[pltpu.VMEM((nC, d, d), F32), pltpu.VMEM((nC, d, d), F32)],
        compiler_params=_params(("parallel", "parallel")),
        name="hgrn",
    )(proj_std, proj_std, proj_std, proj_std, lb, norm_g)


def _merge_kernel(x_ref, yat_ref, yb_ref, ga_ref, gb_ref, gt_ref, sc_ref, sh_ref, lg_ref, lb_ref,
                  wa_ref, wb_ref, wo_ref, x1_ref, h2_ref, *, alpha):
    tm = x_ref.shape[0]
    at = jnp.dot(wa_ref[...], yat_ref[...], preferred_element_type=F32)
    bt = lax.dot_general(wb_ref[...], yb_ref[...], NT_DIMS, preferred_element_type=F32)
    mixed = (jax.nn.sigmoid(ga_ref[...].astype(F32)) * at
             + jax.nn.sigmoid(gb_ref[...].astype(F32)) * bt).astype(BF16)
    ot = jnp.dot(wo_ref[...], mixed, preferred_element_type=F32)
    r = alpha * x_ref[...].T + _rep(_col(gt_ref[...]), tm) * ot
    mu = jnp.mean(r, axis=0, keepdims=True)
    dlt = r - mu
    var = jnp.mean(dlt * dlt, axis=0, keepdims=True)
    x1 = dlt * lax.rsqrt(var + LN_EPS) * _rep(_col(lg_ref[...]), tm) + _rep(_col(lb_ref[...]), tm)
    x1_ref[...] = x1
    h2 = x1 * (1.0 + _rep(_col(sc_ref[...]), tm)) + _rep(_col(sh_ref[...]), tm)
    h2_ref[...] = h2.astype(BF16)


def _merge(x2, ya_t, yb, proj_t, gt1, sc2, sh2, ln_g, ln_b, wa_t, wb_t, wo_t, S, alpha):
    T, D = x2.shape
    tm = 512
    per_b = S // tm
    wa_w, wb_w = wa_t.shape[1], wb_t.shape[1]
    row = pl.BlockSpec((None, 1, D), lambda i: (i // per_b, 0, 0))
    shared = pl.BlockSpec((1, D), lambda i: (0, 0))
    return pl.pallas_call(
        functools.partial(_merge_kernel, alpha=alpha),
        out_shape=(jax.ShapeDtypeStruct((D, T), F32), jax.ShapeDtypeStruct((D, T), BF16)),
        grid=(T // tm,),
        in_specs=[pl.BlockSpec((tm, D), lambda i: (i, 0)),
                  pl.BlockSpec((wa_w, tm), lambda i: (0, i)),
                  pl.BlockSpec((tm, wb_w), lambda i: (i, 0)),
                  pl.BlockSpec((D, tm), lambda i: (0, i)),
                  pl.BlockSpec((D, tm), lambda i: (1, i)),
                  row, row, row, shared, shared,
                  pl.BlockSpec((D, wa_w), lambda i: (0, 0)),
                  pl.BlockSpec((D, wb_w), lambda i: (0, 0)),
                  pl.BlockSpec((D, D), lambda i: (0, 0))],
        out_specs=(pl.BlockSpec((D, tm), lambda i: (0, i)),
                   pl.BlockSpec((D, tm), lambda i: (0, i))),
        compiler_params=_params(("parallel",)),
        name="merge",
    )(x2, ya_t, yb, proj_t, proj_t, gt1, sc2, sh2, ln_g, ln_b, wa_t, wb_t, wo_t)


def _top16(v):
    n = v.shape[1]
    rows = lax.broadcasted_iota(jnp.int32, (PEER_TOPK, n), 0).astype(F32)

    def body(_, carry):
        v, top, cum = carry
        m = v.max(axis=0, keepdims=True)
        eq = v == m
        cnt = jnp.where(eq, 1.0, 0.0).sum(axis=0, keepdims=True)
        top = jnp.where((rows >= cum) & (rows < cum + cnt), m, top)
        return jnp.where(eq, -jnp.inf, v), top, cum + cnt

    _, top, _ = lax.fori_loop(
        0, PEER_TOPK, body,
        (v, jnp.full((PEER_TOPK, n), -jnp.inf, F32), jnp.zeros((1, n), F32)))
    return top


def _kth_pair_sum(a, b):
    K = PEER_TOPK
    r8 = lax.broadcasted_iota(jnp.int32, (8, a.shape[1]), 0)
    cands = [a[0:1] + b[0:8], a[0:1] + b[8:16], a[1:2] + b[0:8]]
    for p in range(2, 8):
        cands.append(jnp.where(r8 < K // (p + 1), a[p:p + 1] + b[0:8], -jnp.inf))
    cands.append(a[8:16] + b[0:1])
    c = jnp.concatenate(cands, axis=0)

    def body(_, carry):
        c, thr, cum = carry
        m = c.max(axis=0, keepdims=True)
        eq = c == m
        new = cum + jnp.where(eq, 1.0, 0.0).sum(axis=0, keepdims=True)
        thr = jnp.where((cum < K) & (new >= K), m, thr)
        return jnp.where(eq, -jnp.inf, c), thr, new

    _, thr, _ = lax.fori_loop(
        0, K, body, (c, jnp.zeros((1, a.shape[1]), F32), jnp.zeros((1, a.shape[1]), F32)))
    return thr


def _peer_kernel(h2_ref, x1_ref, gt_ref, lg_ref, lb_ref, wq_ref, sk_ref, u_ref, vt_ref, o_ref,
                 s_scr, e_scr, thr_scr, st_scr, p_scr, acc_scr, *, alpha):
    e = pl.program_id(1)
    tt = h2_ref.shape[1]
    NK, H = PEER_NKEYS, PEER_HEADS
    eb = u_ref.shape[0]

    @pl.when(e == 0)
    def _prologue():
        qt = jnp.dot(wq_ref[...], h2_ref[...], preferred_element_type=F32).astype(BF16)
        for m in range(2 * H):
            s_scr[m * NK:(m + 1) * NK, :] = jnp.dot(
                sk_ref[m], qt[m * PEER_HALF:(m + 1) * PEER_HALF, :], preferred_element_type=F32)
        for h in range(H):
            for lc in range(tt // LANES):
                cols = slice(lc * LANES, (lc + 1) * LANES)
                s0 = s_scr[(2 * h) * NK:(2 * h + 1) * NK, cols]
                s1 = s_scr[(2 * h + 1) * NK:(2 * h + 2) * NK, cols]
                a = _top16(s0)
                b = _top16(s1)
                thr = _kth_pair_sum(a, b)
                ea = jnp.exp(a - a[0:1])
                eb_ = jnp.exp(b - b[0:1])
                z = jnp.zeros((1, LANES), F32)
                for p in range(PEER_TOPK):
                    z = z + jnp.where(a[p:p + 1] + b >= thr, ea[p:p + 1] * eb_, 0.0).sum(axis=0, keepdims=True)
                e_scr[(2 * h) * NK:(2 * h + 1) * NK, cols] = jnp.exp(s0 - a[0:1]) * (1.0 / z)
                e_scr[(2 * h + 1) * NK:(2 * h + 2) * NK, cols] = jnp.exp(s1 - b[0:1])
                thr_scr[h:h + 1, cols] = thr
        acc_scr[...] = jnp.zeros_like(acc_scr)

    st_scr[...] = jnp.dot(u_ref[...], h2_ref[...], preferred_element_type=F32)
    lw = 2 * LANES

    def iblock(ii, carry):
        i = e * (eb // NK) + ii
        r0 = pl.multiple_of(ii * NK, NK)
        for lc in range(tt // lw):
            cols = slice(lc * lw, (lc + 1) * lw)
            w = jnp.zeros((NK, lw), F32)
            for h in range(H):
                s0 = s_scr[pl.ds((2 * h) * NK + i, 1), cols]
                e0 = e_scr[pl.ds((2 * h) * NK + i, 1), cols]
                s1 = s_scr[(2 * h + 1) * NK:(2 * h + 2) * NK, cols]
                e1 = e_scr[(2 * h + 1) * NK:(2 * h + 2) * NK, cols]
                w = w + jnp.where(s0 + s1 >= thr_scr[h:h + 1, cols], e1, 0.0) * e0
            sv = st_scr[pl.ds(r0, NK), cols]
            g = 0.5 * sv * (1.0 + lax.erf(sv * (2.0 ** -0.5)))
            p_scr[pl.ds(r0, NK), cols] = (w * g).astype(BF16)
        return carry

    lax.fori_loop(0, eb // NK, iblock, 0)
    acc_scr[...] += jnp.dot(vt_ref[...], p_scr[...], preferred_element_type=F32)

    @pl.when(e == pl.num_programs(1) - 1)
    def _epilogue():
        r = alpha * x1_ref[...] + _rep(_col(gt_ref[...]), tt) * acc_scr[...]
        mu = jnp.mean(r, axis=0, keepdims=True)
        dlt = r - mu
        var = jnp.mean(dlt * dlt, axis=0, keepdims=True)
        y = dlt * lax.rsqrt(var + LN_EPS) * _rep(_col(lg_ref[...]), tt) + _rep(_col(lb_ref[...]), tt)
        o_ref[...] = y.T


def _peer(h2_t, x1_t, gt2, ln_g, ln_b, wq_t, sk, u, v_t, S, alpha):
    D, T = h2_t.shape
    tt = 512
    eb = 1024
    per_b = S // tt
    n_exp = u.shape[0]
    H, NK = PEER_HEADS, PEER_NKEYS
    shared = pl.BlockSpec((1, D), lambda t, e: (0, 0))
    return pl.pallas_call(
        functools.partial(_peer_kernel, alpha=alpha),
        out_shape=jax.ShapeDtypeStruct((T, D), F32),
        grid=(T // tt, n_exp // eb),
        in_specs=[pl.BlockSpec((D, tt), lambda t, e: (0, t)),
                  pl.BlockSpec((D, tt), lambda t, e: (0, t)),
                  pl.BlockSpec((None, 1, D), lambda t, e: (t // per_b, 0, 0)),
                  shared, shared,
                  pl.BlockSpec(wq_t.shape, lambda t, e: (0, 0)),
                  pl.BlockSpec(sk.shape, lambda t, e: (0, 0, 0)),
                  pl.BlockSpec((eb, D), lambda t, e: (e, 0)),
                  pl.BlockSpec((D, eb), lambda t, e: (0, e))],
        out_specs=pl.BlockSpec((tt, D), lambda t, e: (t, 0)),
        scratch_shapes=[pltpu.VMEM((2 * H * NK, tt), F32),
                        pltpu.VMEM((2 * H * NK, tt), F32),
                        pltpu.VMEM((H, tt), F32),
                        pltpu.VMEM((eb, tt), F32),
                        pltpu.VMEM((eb, tt), BF16),
                        pltpu.VMEM((D, tt), F32)],
        compiler_params=_params(("parallel", "arbitrary")),
        name="peer",
    )(h2_t, x1_t, gt2, ln_g, ln_b, wq_t, sk, u, v_t)


def kernel(x, c, w_ada, b_ada, w_in, w_branch_a, w_branch_b, w_out, hgrn_lb, hgrn_norm_g,
           ln1_g, ln1_b, peer_wq, peer_subkeys, peer_u, peer_v, ln2_g, ln2_b):
    B, S, D = x.shape
    depth = w_in.shape[0]
    assert depth == 1 and S % 512 == 0 and S <= 4096
    alpha = (2.0 * depth) ** 0.25
    T = B * S
    mw = MOBA_HEADS * MOBA_HEAD_DIM
    hw = HGRN_HEADS * HGRN_DIM

    ada = _ada(c, w_ada[0], b_ada[0])
    sh1, sc1, gt1, sh2, sc2, gt2 = [ada[:, i * D:(i + 1) * D].reshape(B, 1, D) for i in range(6)]

    w = w_in[0]
    o_v, o_h, o_g = 2 * mw, 3 * mw, 3 * mw + 4 * hw
    w_std = jnp.concatenate([w[:, :o_v], w[:, o_h:o_g]], axis=1).astype(BF16)
    w_tr = jnp.concatenate([w[:, o_g:], w[:, o_v:o_h]], axis=1).T.astype(BF16)
    x2 = x.reshape(T, D)
    proj_std, proj_t = _in_proj(x2, sc1, sh1, w_std, w_tr, S)

    ya_t = _moba(proj_std, proj_t, B, S, q_blk=0, k_blk=mw // LANES, v_blk=2 * D // LANES)
    yb = _hgrn(proj_std, hgrn_lb, hgrn_norm_g, B, S, q_blk=2 * mw // LANES)

    x1_t, h2_t = _merge(x2, ya_t, yb, proj_t, gt1, sc2, sh2, ln1_g, ln1_b,
                        w_branch_a[0].T.astype(BF16), w_branch_b[0].T.astype(BF16),
                        w_out[0].T.astype(BF16), S, alpha)

    sk = peer_subkeys[0].reshape(2 * PEER_HEADS, PEER_NKEYS, PEER_HALF).astype(BF16)
    out = _peer(h2_t, x1_t, gt2, ln2_g, ln2_b, peer_wq[0].T.astype(BF16), sk,
                peer_u[0].astype(BF16), peer_v[0].T.astype(BF16), S, alpha)
    return out.reshape(B, S, D)
```

```python
import functools

import jax
import jax.numpy as jnp
from jax import lax
from jax.experimental import pallas as pl
from jax.experimental.pallas import tpu as pltpu

F32 = jnp.float32
BF16 = jnp.bfloat16

LANES = 128
VMEM_LIMIT = 56 * 1024 * 1024

MOBA_HEADS = 8
MOBA_HEAD_DIM = 64
MOBA_BLOCK = 256
MOBA_TOPK = 3
HGRN_HEADS = 4
HGRN_DIM = 128
HGRN_CHUNK = 64
PEER_HEADS = 8
PEER_NKEYS = 128
PEER_HALF = 64
PEER_TOPK = 16
LN_EPS = 1e-5
RMS_EPS = 1e-6
NEG = -1e30

NT_DIMS = (((1,), (1,)), ((), ()))


def _params(sem):
    return pltpu.CompilerParams(dimension_semantics=sem, vmem_limit_bytes=VMEM_LIMIT)


def _silu(v):
    return v * jax.nn.sigmoid(v)


def _col(row):
    return jnp.broadcast_to(row, (LANES, row.shape[1])).T


def _rep(col, n):
    return jnp.concatenate([col] * (n // LANES), axis=1)


def _ada_kernel(c_ref, w_ref, b_ref, o_ref):
    sc = _silu(c_ref[...])
    o_ref[...] = jnp.dot(sc, w_ref[...], precision=lax.Precision.HIGHEST,
                         preferred_element_type=F32) + b_ref[...]


def _ada(c, w, b):
    B, D = c.shape
    N = w.shape[1]
    tn = 1536
    return pl.pallas_call(
        _ada_kernel,
        out_shape=jax.ShapeDtypeStruct((B, N), F32),
        grid=(N // tn,),
        in_specs=[pl.BlockSpec((B, D), lambda j: (0, 0)),
                  pl.BlockSpec((D, tn), lambda j: (0, j)),
                  pl.BlockSpec((1, tn), lambda j: (0, j))],
        out_specs=pl.BlockSpec((B, tn), lambda j: (0, j)),
        compiler_params=_params(("arbitrary",)),
        name="ada",
    )(c, w, b.reshape(1, N))


def _in_proj_kernel(x_ref, sc_ref, sh_ref, ws_ref, wt_ref, os_ref, ot_ref):
    h = (x_ref[...] * (1.0 + sc_ref[...]) + sh_ref[...]).astype(BF16)
    cw = 512
    for j in range(ws_ref.shape[1] // cw):
        os_ref[:, j * cw:(j + 1) * cw] = jnp.dot(
            h, ws_ref[:, j * cw:(j + 1) * cw], preferred_element_type=F32).astype(BF16)
    for j in range(wt_ref.shape[0] // cw):
        ot_ref[j * cw:(j + 1) * cw, :] = lax.dot_general(
            wt_ref[j * cw:(j + 1) * cw, :], h, NT_DIMS, preferred_element_type=F32).astype(BF16)


def _in_proj(x2, sc, sh, w_std, w_tr, S):
    T, D = x2.shape
    ns, nt = w_std.shape[1], w_tr.shape[0]
    tm = 512
    per_b = S // tm
    return pl.pallas_call(
        _in_proj_kernel,
        out_shape=(jax.ShapeDtypeStruct((T, ns), BF16), jax.ShapeDtypeStruct((nt, T), BF16)),
        grid=(T // tm,),
        in_specs=[pl.BlockSpec((tm, D), lambda i: (i, 0)),
                  pl.BlockSpec((None, 1, D), lambda i: (i // per_b, 0, 0)),
                  pl.BlockSpec((None, 1, D), lambda i: (i // per_b, 0, 0)),
                  pl.BlockSpec((D, ns), lambda i: (0, 0)),
                  pl.BlockSpec((nt, D), lambda i: (0, 0))],
        out_specs=(pl.BlockSpec((tm, ns), lambda i: (i, 0)),
                   pl.BlockSpec((nt, tm), lambda i: (0, i))),
        compiler_params=_params(("parallel",)),
        name="in_proj",
    )(x2, sc, sh, w_std, w_tr)


def _moba_kernel(q_ref, k_ref, vt_ref, o_ref):
    S = q_ref.shape[0]
    BS, dh = MOBA_BLOCK, MOBA_HEAD_DIM
    nb = S // BS
    hp = pl.program_id(1)
    q2 = q_ref[...]
    k2 = k_ref[...]
    kmean = jnp.mean(k2.astype(F32).reshape(nb, BS, LANES), axis=1)
    lane = lax.broadcasted_iota(jnp.int32, (1, LANES), 1)
    krow = lax.broadcasted_iota(jnp.int32, (S, LANES), 0)
    pos_hi = (krow >> 4).astype(F32)
    pos_lo = (krow & 15).astype(F32)
    nidx = lax.broadcasted_iota(jnp.int32, (nb, S), 0)
    qblk = lax.broadcasted_iota(jnp.int32, (nb, S), 1) // BS
    past = nidx < qblk
    causal = (lax.broadcasted_iota(jnp.int32, (BS, BS), 0)
              <= lax.broadcasted_iota(jnp.int32, (BS, BS), 1))

    for hh in range(2):
        head = (hp * 2 + hh).astype(F32)
        slope = jnp.exp2(jnp.full((1, LANES), -8.0 / MOBA_HEADS, F32) * (head + 1.0))
        own = (lane >= hh * dh) & (lane < (hh + 1) * dh)
        f0 = (1 - hh) * dh
        kh = jnp.where(own, k2.astype(F32),
                       jnp.where(lane == f0, pos_hi, jnp.where(lane == f0 + 1, pos_lo, 0.0))
                       ).astype(BF16)
        qh = jnp.where(own, q2.astype(F32) * (dh ** -0.5),
                       jnp.where(lane == f0, 16.0 * slope, jnp.where(lane == f0 + 1, slope, 0.0))
                       ).astype(BF16)
        kmh = jnp.where(own, kmean, 0.0).astype(BF16)
        gate = lax.dot_general(kmh, qh, NT_DIMS, preferred_element_type=F32)
        gate = jnp.where(past, gate, -jnp.inf)
        rank = jnp.zeros((nb, S), F32)
        for n2 in range(nb):
            row = gate[n2:n2 + 1, :]
            beats = (row > gate) | ((row == gate) & (n2 < nidx))
            rank = rank + jnp.where(beats, 1.0, 0.0)
        sel = past & (rank < float(MOBA_TOPK))

        for j in range(nb):
            qj = qh[j * BS:(j + 1) * BS, :]
            st = lax.dot_general(kh[:(j + 1) * BS, :], qj, NT_DIMS,
                                 preferred_element_type=F32)
            blocks = []
            for n in range(j):
                keep = sel[n:n + 1, j * BS:(j + 1) * BS]
                blocks.append(jnp.where(keep, st[n * BS:(n + 1) * BS, :], NEG))
            blocks.append(jnp.where(causal, st[j * BS:(j + 1) * BS, :], NEG))
            m = blocks[0].max(axis=0, keepdims=True)
            for blk in blocks[1:]:
                m = jnp.maximum(m, blk.max(axis=0, keepdims=True))
            l = jnp.zeros((1, BS), F32)
            acc = jnp.zeros((LANES, BS), F32)
            for n, blk in enumerate(blocks):
                p = jnp.exp(blk - m)
                l = l + p.sum(axis=0, keepdims=True)
                acc = acc + jnp.dot(vt_ref[:, n * BS:(n + 1) * BS], p.astype(BF16),
                                    preferred_element_type=F32)
            o = acc[hh * dh:(hh + 1) * dh, :] * (1.0 / l)
            o_ref[hh * dh:(hh + 1) * dh, j * BS:(j + 1) * BS] = o.astype(BF16)


def _moba(proj_std, proj_t, B, S, q_blk, k_blk, v_blk):
    T = B * S
    width = MOBA_HEADS * MOBA_HEAD_DIM
    npair = width // LANES
    return pl.pallas_call(
        _moba_kernel,
        out_shape=jax.ShapeDtypeStruct((width, T), BF16),
        grid=(B, npair),
        in_specs=[pl.BlockSpec((S, LANES), lambda b, h: (b, q_blk + h)),
                  pl.BlockSpec((S, LANES), lambda b, h: (b, k_blk + h)),
                  pl.BlockSpec((LANES, S), lambda b, h: (v_blk + h, b))],
        out_specs=pl.BlockSpec((LANES, S), lambda b, h: (h, b)),
        compiler_params=_params(("parallel", "parallel")),
        name="moba",
    )(proj_std, proj_std, proj_t)


def _hgrn_kernel(q_ref, f_ref, i_ref, g_ref, lb_ref, ng_ref, o_ref, ds_scr, st_scr):
    S, d = q_ref.shape
    C = HGRN_CHUNK
    nC = S // C
    lbl = lb_ref[...]
    e = jnp.exp(lbl - lbl.max(axis=0, keepdims=True))
    lb = e[0:1, :] / e.sum(axis=0, keepdims=True)
    f = lb + (1.0 - lb) * jax.nn.sigmoid(f_ref[...].astype(F32))
    lf = jnp.log(f)
    pos = lax.broadcasted_iota(jnp.int32, (S, d), 0) & (C - 1)
    b = lf
    sh = 1
    while sh < C:
        b = b + jnp.where(pos >= sh, pltpu.roll(b, sh, axis=0), 0.0)
        sh *= 2
    b3 = b.reshape(nC, C, d)
    bref = b3[:, C // 2:C // 2 + 1, :]
    blast = b3[:, C - 1:C, :]
    q3 = _silu(q_ref[...].astype(F32)).reshape(nC, C, d)
    k3 = (1.0 - f).reshape(nC, C, d)
    v3 = i_ref[...].reshape(nC, C, d)
    qe = (q3 * jnp.exp(b3 - bref)).astype(BF16)
    ke = (k3 * jnp.exp(bref - b3)).astype(BF16)
    a = jnp.einsum('ctd,csd->cts', qe, ke, preferred_element_type=F32)
    tril = (lax.broadcasted_iota(jnp.int32, (C, C), 0) >= lax.broadcasted_iota(jnp.int32, (C, C), 1))
    a = jnp.where(tril[None], a, 0.0).astype(BF16)
    o_intra = jnp.einsum('cts,csv->ctv', a, v3, preferred_element_type=F32)
    kd = (k3 * jnp.exp(blast - b3)).astype(BF16)
    v3t = jnp.swapaxes(v3.astype(F32), 1, 2).astype(BF16)
    ds_scr[...] = jnp.einsum('cvs,csd->cvd', v3t, kd, preferred_element_type=F32)
    decay = jnp.exp(blast)

    st = jnp.zeros((d, d), F32)
    for c in range(nC):
        st_scr[c] = st
        st = st * decay[c] + ds_scr[c]
    qb = (q3 * jnp.exp(b3)).astype(BF16)
    o_inter = jnp.einsum('ctd,cvd->ctv', qb, st_scr[...].astype(BF16), preferred_element_type=F32)
    o = o_intra + o_inter
    o = o * lax.rsqrt(jnp.mean(o * o, axis=-1, keepdims=True) + RMS_EPS)
    o = o.reshape(S, d) * ng_ref[...] * _silu(g_ref[...].astype(F32))
    o_ref[...] = o.astype(BF16)


def _hgrn(proj_std, lb, norm_g, B, S, q_blk):
    T = B * S
    H, d = HGRN_HEADS, HGRN_DIM
    nC = S // HGRN_CHUNK

    def col(off):
        return pl.BlockSpec((S, d), lambda b, h: (b, q_blk + off * H + h))

    return pl.pallas_call(
        _hgrn_kernel,
        out_shape=jax.ShapeDtypeStruct((T, H * d), BF16),
        grid=(B, H),
        in_specs=[col(0), col(1), col(2), col(3),
                  pl.BlockSpec((lb.shape[0], d), lambda b, h: (0, h)),
                  pl.BlockSpec((1, d), lambda b, h: (0, h))],
        out_specs=pl.BlockSpec((S, d), lambda b, h: (b, h)),
        scratch_shapes=[pltpu.VMEM((nC, d, d), F32), pltpu.VMEM((nC, d, d), F32)],
        compiler_params=_params(("parallel", "parallel")),
        name="hgrn",
    )(proj_std, proj_std, proj_std, proj_std, lb, norm_g)


def _merge_kernel(x_ref, yat_ref, yb_ref, ga_ref, gb_ref, gt_ref, sc_ref, sh_ref, lg_ref, lb_ref,
                  wa_ref, wb_ref, wo_ref, x1_ref, h2_ref, *, alpha):
    tm = x_ref.shape[0]
    at = jnp.dot(wa_ref[...], yat_ref[...], preferred_element_type=F32)
    bt = lax.dot_general(wb_ref[...], yb_ref[...], NT_DIMS, preferred_element_type=F32)
    mixed = (jax.nn.sigmoid(ga_ref[...].astype(F32)) * at
             + jax.nn.sigmoid(gb_ref[...].astype(F32)) * bt).astype(BF16)
    ot = jnp.dot(wo_ref[...], mixed, preferred_element_type=F32)
    r = alpha * x_ref[...].T + _rep(_col(gt_ref[...]), tm) * ot
    mu = jnp.mean(r, axis=0, keepdims=True)
    dlt = r - mu
    var = jnp.mean(dlt * dlt, axis=0, keepdims=True)
    x1 = dlt * lax.rsqrt(var + LN_EPS) * _rep(_col(lg_ref[...]), tm) + _rep(_col(lb_ref[...]), tm)
    x1_ref[...] = x1
    h2 = x1 * (1.0 + _rep(_col(sc_ref[...]), tm)) + _rep(_col(sh_ref[...]), tm)
    h2_ref[...] = h2.astype(BF16)


def _merge(x2, ya_t, yb, proj_t, gt1, sc2, sh2, ln_g, ln_b, wa_t, wb_t, wo_t, S, alpha):
    T, D = x2.shape
    tm = 512
    per_b = S // tm
    wa_w, wb_w = wa_t.shape[1], wb_t.shape[1]
    row = pl.BlockSpec((None, 1, D), lambda i: (i // per_b, 0, 0))
    shared = pl.BlockSpec((1, D), lambda i: (0, 0))
    return pl.pallas_call(
        functools.partial(_merge_kernel, alpha=alpha),
        out_shape=(jax.ShapeDtypeStruct((D, T), F32), jax.ShapeDtypeStruct((D, T), BF16)),
        grid=(T // tm,),
        in_specs=[pl.BlockSpec((tm, D), lambda i: (i, 0)),
                  pl.BlockSpec((wa_w, tm), lambda i: (0, i)),
                  pl.BlockSpec((tm, wb_w), lambda i: (i, 0)),
                  pl.BlockSpec((D, tm), lambda i: (0, i)),
                  pl.BlockSpec((D, tm), lambda i: (1, i)),
                  row, row, row, shared, shared,
                  pl.BlockSpec((D, wa_w), lambda i: (0, 0)),
                  pl.BlockSpec((D, wb_w), lambda i: (0, 0)),
                  pl.BlockSpec((D, D), lambda i: (0, 0))],
        out_specs=(pl.BlockSpec((D, tm), lambda i: (0, i)),
                   pl.BlockSpec((D, tm), lambda i: (0, i))),
        compiler_params=_params(("parallel",)),
        name="merge",
    )(x2, ya_t, yb, proj_t, proj_t, gt1, sc2, sh2, ln_g, ln_b, wa_t, wb_t, wo_t)


def _sort_network(n):
    pairs, p = [], 1
    while p < n:
        k = p
        while k >= 1:
            for j in range(k % p, n - k, 2 * k):
                for i in range(min(k, n - j - k)):
                    if (i + j) // (2 * p) == (i + j + k) // (2 * p):
                        pairs.append((i + j, i + j + k))
            k //= 2
        p *= 2
    return pairs


def _exchange(v, i, j):
    v[i], v[j] = jnp.maximum(v[i], v[j]), jnp.minimum(v[i], v[j])


def _top16(v, sort_last=True):
    K = PEER_TOPK
    v = list(v)
    for i, j in _sort_network(len(v)):
        _exchange(v, i, j)
    v = v + [jnp.full_like(v[0], -jnp.inf)] * (K - len(v))
    for shift in (1, 2, 4):
        w = [pltpu.roll(x, shift, axis=0) for x in v]
        v = [jnp.maximum(v[k], w[K - 1 - k]) for k in range(K)]
        if shift < 4 or sort_last:
            d = K // 2
            while d >= 1:
                for k in range(K):
                    if k & d == 0:
                        _exchange(v, k, k + d)
                d //= 2
    return v


def _peer_select(s0, s1):
    K = PEER_TOPK
    v0 = [s0[8 * k:8 * k + 8, :] for k in range(K)]
    v1 = [s1[8 * k:8 * k + 8, :] for k in range(K)]
    a = _top16(v0)
    b = _top16(v1)
    sub = lax.broadcasted_iota(jnp.int32, v0[0].shape, 0)

    def pack(vals):
        out = vals[7]
        for r in range(6, -1, -1):
            out = jnp.where(sub == r, vals[r], out)
        return out

    b_lo, b_hi, a_hi = pack(b[0:8]), pack(b[8:16]), pack(a[8:16])
    cands = [a[0] + b_lo, a[0] + b_hi, a[1] + b_lo]
    for p in range(2, 8):
        cands.append(jnp.where(sub < K // (p + 1), a[p] + b_lo, -jnp.inf))
    cands.append(a_hi + b[0])
    top = _top16(cands, sort_last=False)
    thr = top[0]
    for t in top[1:]:
        thr = jnp.minimum(thr, t)
    ea = [jnp.exp(x - a[0]) for x in a]
    eb = [jnp.exp(x - b[0]) for x in b]
    eb_lo, eb_hi = pack(eb[0:8]), pack(eb[8:16])
    z = jnp.zeros_like(thr)
    for p in range(K):
        z = z + jnp.where(a[p] + b_lo >= thr, ea[p] * eb_lo, 0.0)
        z = z + jnp.where(a[p] + b_hi >= thr, ea[p] * eb_hi, 0.0)
    for shift in (4, 2, 1):
        z = z + pltpu.roll(z, shift, axis=0)
    inv_z = 1.0 / z
    count0, e0, rank1, e1 = [], [], [], []
    for k in range(K):
        cnt = jnp.zeros_like(thr)
        for r in range(K):
            cnt = cnt + jnp.where(v0[k] + b[r] >= thr, 1.0, 0.0)
        count0.append(cnt)
        e0.append(jnp.exp(v0[k] - a[0]) * inv_z)
        rk = jnp.full_like(thr, float(K))
        for r in range(K - 1, -1, -1):
            rk = jnp.where(v1[k] >= b[r], float(r), rk)
        rank1.append(rk)
        e1.append(jnp.exp(v1[k] - b[0]))
    cat = lambda parts: jnp.concatenate(parts, axis=0)
    return cat(count0), cat(e0), cat(rank1), cat(e1)


def _peer_kernel(h2_ref, x1_ref, gt_ref, lg_ref, lb_ref, wq_ref, sk_ref, u0_ref, ub_ref, ua_ref, vt_ref,
                 o_ref, s_scr, cnt_scr, e0_scr, re_scr, sta_scr, stb_scr, p_scr, acc_scr, *, alpha):
    e = pl.program_id(1)
    tt = h2_ref.shape[1]
    NK, H = PEER_NKEYS, PEER_HEADS
    half = ub_ref.shape[0]
    eb = 2 * half
    LC = tt // LANES
    BT = 16

    @pl.when(e == 0)
    def _prologue():
        qt = jnp.dot(wq_ref[...], h2_ref[...], preferred_element_type=F32).astype(BF16)
        for m in range(2 * H):
            sm = jnp.dot(sk_ref[m], qt[m * PEER_HALF:(m + 1) * PEER_HALF, :],
                         preferred_element_type=F32)
            for lc in range(LC):
                s_scr[lc, m * NK:(m + 1) * NK, :] = sm[:, lc * LANES:(lc + 1) * LANES]

        def chunk(lc, carry):
            for h in range(H):
                c0, e0, r1, e1 = _peer_select(s_scr[lc, (2 * h) * NK:(2 * h + 1) * NK, :],
                                              s_scr[lc, (2 * h + 1) * NK:(2 * h + 2) * NK, :])
                cnt_scr[lc, h * NK:(h + 1) * NK, :] = c0
                e0_scr[lc, h * NK:(h + 1) * NK, :] = e0
                t0 = h * (NK // BT)
                re_scr[lc, t0:t0 + NK // BT, 0] = r1.astype(BF16).reshape(NK // BT, BT, LANES)
                re_scr[lc, t0:t0 + NK // BT, 1] = e1.astype(BF16).reshape(NK // BT, BT, LANES)
            return carry

        lax.fori_loop(0, LC, chunk, 0)
        acc_scr[...] = jnp.zeros_like(acc_scr)
        sta_scr[...] = jnp.dot(u0_ref[...], h2_ref[...], preferred_element_type=F32)

    def gate_block(ii, st_scr, r0):
        i = e * (eb // NK) + ii
        for lc in range(LC):
            w = jnp.zeros((NK // BT, BT, LANES), BF16)
            for h in range(H):
                cnt = cnt_scr[lc, pl.ds(h * NK + i, 1), :]
                e0 = e0_scr[lc, pl.ds(h * NK + i, 1), :]
                cnt = jnp.broadcast_to(cnt, (BT, LANES)).astype(BF16)
                e0 = jnp.broadcast_to(e0, (BT, LANES)).astype(BF16)
                re = re_scr[lc, h * (NK // BT):(h + 1) * (NK // BT)]
                rk, e1 = re[:, 0], re[:, 1]
                w = w + jnp.where(rk < cnt[None], e1, jnp.zeros_like(e1)) * e0[None]
            sv = st_scr[r0:r0 + NK, lc * LANES:(lc + 1) * LANES]
            g = (0.5 * sv * (1.0 + lax.erf(sv * (2.0 ** -0.5)))).astype(BF16)
            p_scr[ii * NK:(ii + 1) * NK, lc * LANES:(lc + 1) * LANES] = w.reshape(NK, LANES) * g

    def second(c0, n):
        return jnp.dot(vt_ref[:, c0:c0 + n], p_scr[c0:c0 + n, :], preferred_element_type=F32)

    h2 = h2_ref[...]
    nblk = half // NK
    gate_block(0, sta_scr, 0)
    stb_scr[...] = jnp.dot(ub_ref[...], h2, preferred_element_type=F32)
    out = None
    for k in range(1, nblk):
        gate_block(k, sta_scr, k * NK)
        if k % 2 == 1:
            part = second((k - 1) * NK, 2 * NK)
            out = part if out is None else out + part
    gate_block(nblk, stb_scr, 0)
    sta_scr[...] = jnp.dot(ua_ref[...], h2, preferred_element_type=F32)
    for k in range(1, nblk):
        gate_block(nblk + k, stb_scr, k * NK)
        if k % 2 == 1:
            out = out + second((nblk + k - 1) * NK, 2 * NK)
    acc_scr[...] += out

    @pl.when(e == pl.num_programs(1) - 1)
    def _epilogue():
        r = alpha * x1_ref[...] + _rep(_col(gt_ref[...]), tt) * acc_scr[...]
        mu = jnp.mean(r, axis=0, keepdims=True)
        dlt = r - mu
        var = jnp.mean(dlt * dlt, axis=0, keepdims=True)
        y = dlt * lax.rsqrt(var + LN_EPS) * _rep(_col(lg_ref[...]), tt) + _rep(_col(lb_ref[...]), tt)
        o_ref[...] = y.T


def _peer(h2_t, x1_t, gt2, ln_g, ln_b, wq_t, sk, u, v_t, S, alpha):
    D, T = h2_t.shape
    tt = 512
    eb = 1024
    per_b = S // tt
    half = eb // 2
    n_half = u.shape[0] // half
    n_exp = u.shape[0]
    H, NK = PEER_HEADS, PEER_NKEYS
    shared = pl.BlockSpec((1, D), lambda t, e: (0, 0))
    return pl.pallas_call(
        functools.partial(_peer_kernel, alpha=alpha),
        out_shape=jax.ShapeDtypeStruct((T, D), F32),
        grid=(T // tt, n_exp // eb),
        in_specs=[pl.BlockSpec((D, tt), lambda t, e: (0, t)),
                  pl.BlockSpec((D, tt), lambda t, e: (0, t)),
                  pl.BlockSpec((None, 1, D), lambda t, e: (t // per_b, 0, 0)),
                  shared, shared,
                  pl.BlockSpec(wq_t.shape, lambda t, e: (0, 0)),
                  pl.BlockSpec(sk.shape, lambda t, e: (0, 0, 0)),
                  pl.BlockSpec((half, D), lambda t, e: (0, 0)),
                  pl.BlockSpec((half, D), lambda t, e: (2 * e + 1, 0)),
                  pl.BlockSpec((half, D), lambda t, e: (jnp.minimum(2 * e + 2, n_half - 1), 0)),
                  pl.BlockSpec((D, eb), lambda t, e: (0, e))],
        out_specs=pl.BlockSpec((tt, D), lambda t, e: (t, 0)),
        scratch_shapes=[pltpu.VMEM((tt // LANES, 2 * H * NK, LANES), F32),
                        pltpu.VMEM((tt // LANES, H * NK, LANES), F32),
                        pltpu.VMEM((tt // LANES, H * NK, LANES), F32),
                        pltpu.VMEM((tt // LANES, H * NK // 16, 2, 16, LANES), BF16),
                        pltpu.VMEM((half, tt), F32),
                        pltpu.VMEM((half, tt), F32),
                        pltpu.VMEM((eb, tt), BF16),
                        pltpu.VMEM((D, tt), F32)],
        compiler_params=_params(("parallel", "arbitrary")),
        name="peer",
    )(h2_t, x1_t, gt2, ln_g, ln_b, wq_t, sk, u, u, u, v_t)


def kernel(x, c, w_ada, b_ada, w_in, w_branch_a, w_branch_b, w_out, hgrn_lb, hgrn_norm_g,
           ln1_g, ln1_b, peer_wq, peer_subkeys, peer_u, peer_v, ln2_g, ln2_b):
    B, S, D = x.shape
    depth = w_in.shape[0]
    assert depth == 1 and S % 512 == 0 and S <= 4096
    alpha = (2.0 * depth) ** 0.25
    T = B * S
    mw = MOBA_HEADS * MOBA_HEAD_DIM
    hw = HGRN_HEADS * HGRN_DIM

    ada = _ada(c, w_ada[0], b_ada[0])
    sh1, sc1, gt1, sh2, sc2, gt2 = [ada[:, i * D:(i + 1) * D].reshape(B, 1, D) for i in range(6)]

    w = w_in[0]
    o_v, o_h, o_g = 2 * mw, 3 * mw, 3 * mw + 4 * hw
    w_std = jnp.concatenate([w[:, :o_v], w[:, o_h:o_g]], axis=1).astype(BF16)
    w_tr = jnp.concatenate([w[:, o_g:], w[:, o_v:o_h]], axis=1).T.astype(BF16)
    x2 = x.reshape(T, D)
    proj_std, proj_t = _in_proj(x2, sc1, sh1, w_std, w_tr, S)

    ya_t = _moba(proj_std, proj_t, B, S, q_blk=0, k_blk=mw // LANES, v_blk=2 * D // LANES)
    yb = _hgrn(proj_std, hgrn_lb, hgrn_norm_g, B, S, q_blk=2 * mw // LANES)

    x1_t, h2_t = _merge(x2, ya_t, yb, proj_t, gt1, sc2, sh2, ln1_g, ln1_b,
                        w_branch_a[0].T.astype(BF16), w_branch_b[0].T.astype(BF16),
                        w_out[0].T.astype(BF16), S, alpha)

    sk = peer_subkeys[0].reshape(2 * PEER_HEADS, PEER_NKEYS, PEER_HALF).astype(BF16)
    out = _peer(h2_t, x1_t, gt2, ln2_g, ln2_b, peer_wq[0].T.astype(BF16), sk,
                peer_u[0].astype(BF16), peer_v[0].T.astype(BF16), S, alpha)
    return out.reshape(B, S, D)
```

```python
import functools

import jax
import jax.numpy as jnp
from jax import lax
from jax.experimental import pallas as pl
from jax.experimental.pallas import tpu as pltpu

F32 = jnp.float32
BF16 = jnp.bfloat16

LANES = 128
VMEM_LIMIT = 56 * 1024 * 1024

MOBA_HEADS = 8
MOBA_HEAD_DIM = 64
MOBA_BLOCK = 256
MOBA_TOPK = 3
HGRN_HEADS = 4
HGRN_DIM = 128
HGRN_CHUNK = 64
PEER_HEADS = 8
PEER_NKEYS = 128
PEER_HALF = 64
PEER_TOPK = 16
LN_EPS = 1e-5
RMS_EPS = 1e-6
NEG = -1e30

NT_DIMS = (((1,), (1,)), ((), ()))


def _params(sem):
    return pltpu.CompilerParams(dimension_semantics=sem, vmem_limit_bytes=VMEM_LIMIT)


def _silu(v):
    return v * jax.nn.sigmoid(v)


def _col(row):
    return jnp.broadcast_to(row, (LANES, row.shape[1])).T


def _rep(col, n):
    return jnp.concatenate([col] * (n // LANES), axis=1)


def _ada_kernel(c_ref, w_ref, b_ref, o_ref):
    sc = _silu(c_ref[...])
    o_ref[...] = jnp.dot(sc, w_ref[...], precision=lax.Precision.HIGHEST,
                         preferred_element_type=F32) + b_ref[...]


def _ada(c, w, b):
    B, D = c.shape
    N = w.shape[1]
    tn = 1536
    return pl.pallas_call(
        _ada_kernel,
        out_shape=jax.ShapeDtypeStruct((B, N), F32),
        grid=(N // tn,),
        in_specs=[pl.BlockSpec((B, D), lambda j: (0, 0)),
                  pl.BlockSpec((D, tn), lambda j: (0, j)),
                  pl.BlockSpec((1, tn), lambda j: (0, j))],
        out_specs=pl.BlockSpec((B, tn), lambda j: (0, j)),
        compiler_params=_params(("arbitrary",)),
        name="ada",
    )(c, w, b.reshape(1, N))


def _in_proj_kernel(x_ref, sc_ref, sh_ref, ws_ref, wt_ref, os_ref, ot_ref):
    h = (x_ref[...] * (1.0 + sc_ref[...]) + sh_ref[...]).astype(BF16)
    cw = 512
    for j in range(ws_ref.shape[1] // cw):
        os_ref[:, j * cw:(j + 1) * cw] = jnp.dot(
            h, ws_ref[:, j * cw:(j + 1) * cw], preferred_element_type=F32).astype(BF16)
    for j in range(wt_ref.shape[0] // cw):
        ot_ref[j * cw:(j + 1) * cw, :] = lax.dot_general(
            wt_ref[j * cw:(j + 1) * cw, :], h, NT_DIMS, preferred_element_type=F32).astype(BF16)


def _in_proj(x2, sc, sh, w_std, w_tr, S):
    T, D = x2.shape
    ns, nt = w_std.shape[1], w_tr.shape[0]
    tm = 512
    per_b = S // tm
    return pl.pallas_call(
        _in_proj_kernel,
        out_shape=(jax.ShapeDtypeStruct((T, ns), BF16), jax.ShapeDtypeStruct((nt, T), BF16)),
        grid=(T // tm,),
        in_specs=[pl.BlockSpec((tm, D), lambda i: (i, 0)),
                  pl.BlockSpec((None, 1, D), lambda i: (i // per_b, 0, 0)),
                  pl.BlockSpec((None, 1, D), lambda i: (i // per_b, 0, 0)),
                  pl.BlockSpec((D, ns), lambda i: (0, 0)),
                  pl.BlockSpec((nt, D), lambda i: (0, 0))],
        out_specs=(pl.BlockSpec((tm, ns), lambda i: (i, 0)),
                   pl.BlockSpec((nt, tm), lambda i: (0, i))),
        compiler_params=_params(("parallel",)),
        name="in_proj",
    )(x2, sc, sh, w_std, w_tr)


def _moba_kernel(q_ref, k_ref, vt_ref, o_ref):
    S = q_ref.shape[0]
    BS, dh = MOBA_BLOCK, MOBA_HEAD_DIM
    nb = S // BS
    hp = pl.program_id(1)
    q2 = q_ref[...]
    k2 = k_ref[...]
    kmean = jnp.mean(k2.astype(F32).reshape(nb, BS, LANES), axis=1)
    lane = lax.broadcasted_iota(jnp.int32, (1, LANES), 1)
    krow = lax.broadcasted_iota(jnp.int32, (S, LANES), 0)
    pos_hi = (krow >> 4).astype(F32)
    pos_lo = (krow & 15).astype(F32)
    nidx = lax.broadcasted_iota(jnp.int32, (nb, S), 0)
    qblk = lax.broadcasted_iota(jnp.int32, (nb, S), 1) // BS
    past = nidx < qblk
    causal = (lax.broadcasted_iota(jnp.int32, (BS, BS), 0)
              <= lax.broadcasted_iota(jnp.int32, (BS, BS), 1))

    for hh in range(2):
        head = (hp * 2 + hh).astype(F32)
        slope = jnp.exp2(jnp.full((1, LANES), -8.0 / MOBA_HEADS, F32) * (head + 1.0))
        own = (lane >= hh * dh) & (lane < (hh + 1) * dh)
        f0 = (1 - hh) * dh
        kh = jnp.where(own, k2.astype(F32),
                       jnp.where(lane == f0, pos_hi, jnp.where(lane == f0 + 1, pos_lo, 0.0))
                       ).astype(BF16)
        qh = jnp.where(own, q2.astype(F32) * (dh ** -0.5),
                       jnp.where(lane == f0, 16.0 * slope, jnp.where(lane == f0 + 1, slope, 0.0))
                       ).astype(BF16)
        kmh = jnp.where(own, kmean, 0.0).astype(BF16)
        gate = lax.dot_general(kmh, qh, NT_DIMS, preferred_element_type=F32)
        gate = jnp.where(past, gate, -jnp.inf)
        rank = jnp.zeros((nb, S), F32)
        for n2 in range(nb):
            row = gate[n2:n2 + 1, :]
            beats = (row > gate) | ((row == gate) & (n2 < nidx))
            rank = rank + jnp.where(beats, 1.0, 0.0)
        sel = past & (rank < float(MOBA_TOPK))

        for j in range(nb):
            qj = qh[j * BS:(j + 1) * BS, :]
            st = lax.dot_general(kh[:(j + 1) * BS, :], qj, NT_DIMS,
                                 preferred_element_type=F32)
            blocks = []
            for n in range(j):
                keep = sel[n:n + 1, j * BS:(j + 1) * BS]
                blocks.append(jnp.where(keep, st[n * BS:(n + 1) * BS, :], NEG))
            blocks.append(jnp.where(causal, st[j * BS:(j + 1) * BS, :], NEG))
            m = blocks[0].max(axis=0, keepdims=True)
            for blk in blocks[1:]:
                m = jnp.maximum(m, blk.max(axis=0, keepdims=True))
            l = jnp.zeros((1, BS), F32)
            acc = jnp.zeros((LANES, BS), F32)
            for n, blk in enumerate(blocks):
                p = jnp.exp(blk - m)
                l = l + p.sum(axis=0, keepdims=True)
                acc = acc + jnp.dot(vt_ref[:, n * BS:(n + 1) * BS], p.astype(BF16),
                                    preferred_element_type=F32)
            o = acc[hh * dh:(hh + 1) * dh, :] * (1.0 / l)
            o_ref[hh * dh:(hh + 1) * dh, j * BS:(j + 1) * BS] = o.astype(BF16)


def _moba(proj_std, proj_t, B, S, q_blk, k_blk, v_blk):
    T = B * S
    width = MOBA_HEADS * MOBA_HEAD_DIM
    npair = width // LANES
    return pl.pallas_call(
        _moba_kernel,
        out_shape=jax.ShapeDtypeStruct((width, T), BF16),
        grid=(B, npair),
        in_specs=[pl.BlockSpec((S, LANES), lambda b, h: (b, q_blk + h)),
                  pl.BlockSpec((S, LANES), lambda b, h: (b, k_blk + h)),
                  pl.BlockSpec((LANES, S), lambda b, h: (v_blk + h, b))],
        out_specs=pl.BlockSpec((LANES, S), lambda b, h: (h, b)),
        compiler_params=_params(("parallel", "parallel")),
        name="moba",
    )(proj_std, proj_std, proj_t)


def _hgrn_kernel(q_ref, f_ref, i_ref, g_ref, lb_ref, ng_ref, o_ref, ds_scr, st_scr):
    S, d = q_ref.shape
    C = HGRN_CHUNK
    nC = S // C
    lbl = lb_ref[...]
    e = jnp.exp(lbl - lbl.max(axis=0, keepdims=True))
    lb = e[0:1, :] / e.sum(axis=0, keepdims=True)
    f = lb + (1.0 - lb) * jax.nn.sigmoid(f_ref[...].astype(F32))
    lf = jnp.log(f)
    pos = lax.broadcasted_iota(jnp.int32, (S, d), 0) & (C - 1)
    b = lf
    sh = 1
    while sh < C:
        b = b + jnp.where(pos >= sh, pltpu.roll(b, sh, axis=0), 0.0)
        sh *= 2
    b3 = b.reshape(nC, C, d)
    bref = b3[:, C // 2:C // 2 + 1, :]
    blast = b3[:, C - 1:C, :]
    q3 = _silu(q_ref[...].astype(F32)).reshape(nC, C, d)
    k3 = (1.0 - f).reshape(nC, C, d)
    v3 = i_ref[...].reshape(nC, C, d)
    qe = (q3 * jnp.exp(b3 - bref)).astype(BF16)
    ke = (k3 * jnp.exp(bref - b3)).astype(BF16)
    a = jnp.einsum('ctd,csd->cts', qe, ke, preferred_element_type=F32)
    tril = (lax.broadcasted_iota(jnp.int32, (C, C), 0) >= lax.broadcasted_iota(jnp.int32, (C, C), 1))
    a = jnp.where(tril[None], a, 0.0).astype(BF16)
    o_intra = jnp.einsum('cts,csv->ctv', a, v3, preferred_element_type=F32)
    kd = (k3 * jnp.exp(blast - b3)).astype(BF16)
    v3t = jnp.swapaxes(v3.astype(F32), 1, 2).astype(BF16)
    ds_scr[...] = jnp.einsum('cvs,csd->cvd', v3t, kd, preferred_element_type=F32)
    decay = jnp.exp(blast)

    st = jnp.zeros((d, d), F32)
    for c in range(nC):
        st_scr[c] = st
        st = st * decay[c] + ds_scr[c]
    qb = (q3 * jnp.exp(b3)).astype(BF16)
    o_inter = jnp.einsum('ctd,cvd->ctv', qb, st_scr[...].astype(BF16), preferred_element_type=F32)
    o = o_intra + o_inter
    o = o * lax.rsqrt(jnp.mean(o * o, axis=-1, keepdims=True) + RMS_EPS)
    o = o.reshape(S, d) * ng_ref[...] * _silu(g_ref[...].astype(F32))
    o_ref[...] = o.astype(BF16)


def _hgrn(proj_std, lb, norm_g, B, S, q_blk):
    T = B * S
    H, d = HGRN_HEADS, HGRN_DIM
    nC = S // HGRN_CHUNK

    def col(off):
        return pl.BlockSpec((S, d), lambda b, h: (b, q_blk + off * H + h))

    return pl.pallas_call(
        _hgrn_kernel,
        out_shape=jax.ShapeDtypeStruct((T, H * d), BF16),
        grid=(B, H),
        in_specs=[col(0), col(1), col(2), col(3),
                  pl.BlockSpec((lb.shape[0], d), lambda b, h: (0, h)),
                  pl.BlockSpec((1, d), lambda b, h: (0, h))],
        out_specs=pl.BlockSpec((S, d), lambda b, h: (b, h)),
        scratch_shapes=[pltpu.VMEM((nC, d, d), F32), pltpu.VMEM((nC, d, d), F32)],
        compiler_params=_params(("parallel", "parallel")),
        name="hgrn",
    )(proj_std, proj_std, proj_std, proj_std, lb, norm_g)


def _merge_kernel(x_ref, yat_ref, yb_ref, ga_ref, gb_ref, gt_ref, sc_ref, sh_ref, lg_ref, lb_ref,
                  wa_ref, wb_ref, wo_ref, x1_ref, h2_ref, *, alpha):
    tm = x_ref.shape[0]
    at = jnp.dot(wa_ref[...], yat_ref[...], preferred_element_type=F32)
    bt = lax.dot_general(wb_ref[...], yb_ref[...], NT_DIMS, preferred_element_type=F32)
    mixed = (jax.nn.sigmoid(ga_ref[...].astype(F32)) * at
             + jax.nn.sigmoid(gb_ref[...].astype(F32)) * bt).astype(BF16)
    ot = jnp.dot(wo_ref[...], mixed, preferred_element_type=F32)
    r = alpha * x_ref[...].T + _rep(_col(gt_ref[...]), tm) * ot
    mu = jnp.mean(r, axis=0, keepdims=True)
    dlt = r - mu
    var = jnp.mean(dlt * dlt, axis=0, keepdims=True)
    x1 = dlt * lax.rsqrt(var + LN_EPS) * _rep(_col(lg_ref[...]), tm) + _rep(_col(lb_ref[...]), tm)
    x1_ref[...] = x1
    h2 = x1 * (1.0 + _rep(_col(sc_ref[...]), tm)) + _rep(_col(sh_ref[...]), tm)
    h2_ref[...] = h2.astype(BF16)


def _merge(x2, ya_t, yb, proj_t, gt1, sc2, sh2, ln_g, ln_b, wa_t, wb_t, wo_t, S, alpha):
    T, D = x2.shape
    tm = 512
    per_b = S // tm
    wa_w, wb_w = wa_t.shape[1], wb_t.shape[1]
    row = pl.BlockSpec((None, 1, D), lambda i: (i // per_b, 0, 0))
    shared = pl.BlockSpec((1, D), lambda i: (0, 0))
    return pl.pallas_call(
        functools.partial(_merge_kernel, alpha=alpha),
        out_shape=(jax.ShapeDtypeStruct((D, T), F32), jax.ShapeDtypeStruct((D, T), BF16)),
        grid=(T // tm,),
        in_specs=[pl.BlockSpec((tm, D), lambda i: (i, 0)),
                  pl.BlockSpec((wa_w, tm), lambda i: (0, i)),
                  pl.BlockSpec((tm, wb_w), lambda i: (i, 0)),
                  pl.BlockSpec((D, tm), lambda i: (0, i)),
                  pl.BlockSpec((D, tm), lambda i: (1, i)),
                  row, row, row, shared, shared,
                  pl.BlockSpec((D, wa_w), lambda i: (0, 0)),
                  pl.BlockSpec((D, wb_w), lambda i: (0, 0)),
                  pl.BlockSpec((D, D), lambda i: (0, 0))],
        out_specs=(pl.BlockSpec((D, tm), lambda i: (0, i)),
                   pl.BlockSpec((D, tm), lambda i: (0, i))),
        compiler_params=_params(("parallel",)),
        name="merge",
    )(x2, ya_t, yb, proj_t, proj_t, gt1, sc2, sh2, ln_g, ln_b, wa_t, wb_t, wo_t)


def _sort_network(n):
    pairs, p = [], 1
    while p < n:
        k = p
        while k >= 1:
            for j in range(k % p, n - k, 2 * k):
                for i in range(min(k, n - j - k)):
                    if (i + j) // (2 * p) == (i + j + k) // (2 * p):
                        pairs.append((i + j, i + j + k))
            k //= 2
        p *= 2
    return pairs


def _exchange(v, i, j):
    v[i], v[j] = jnp.maximum(v[i], v[j]), jnp.minimum(v[i], v[j])


def _top16(v, sort_last=True):
    K = PEER_TOPK
    v = list(v)
    for i, j in _sort_network(len(v)):
        _exchange(v, i, j)
    v = v + [jnp.full_like(v[0], -jnp.inf)] * (K - len(v))
    for shift in (1, 2, 4):
        w = [pltpu.roll(x, shift, axis=0) for x in v]
        v = [jnp.maximum(v[k], w[K - 1 - k]) for k in range(K)]
        if shift < 4 or sort_last:
            d = K // 2
            while d >= 1:
                for k in range(K):
                    if k & d == 0:
                        _exchange(v, k, k + d)
                d //= 2
    return v


def _peer_select(s0, s1):
    K = PEER_TOPK
    v0 = [s0[8 * k:8 * k + 8, :] for k in range(K)]
    v1 = [s1[8 * k:8 * k + 8, :] for k in range(K)]
    a = _top16(v0)
    b = _top16(v1)
    sub = lax.broadcasted_iota(jnp.int32, v0[0].shape, 0)

    def pack(vals):
        out = vals[7]
        for r in range(6, -1, -1):
            out = jnp.where(sub == r, vals[r], out)
        return out

    b_lo, b_hi, a_hi = pack(b[0:8]), pack(b[8:16]), pack(a[8:16])
    cands = [a[0] + b_lo, a[0] + b_hi, a[1] + b_lo]
    for p in range(2, 8):
        cands.append(jnp.where(sub < K // (p + 1), a[p] + b_lo, -jnp.inf))
    cands.append(a_hi + b[0])
    top = _top16(cands, sort_last=False)
    thr = top[0]
    for t in top[1:]:
        thr = jnp.minimum(thr, t)
    ea = [jnp.exp(x - a[0]) for x in a]
    eb = [jnp.exp(x - b[0]) for x in b]
    eb_lo, eb_hi = pack(eb[0:8]), pack(eb[8:16])
    z = jnp.zeros_like(thr)
    for p in range(K):
        z = z + jnp.where(a[p] + b_lo >= thr, ea[p] * eb_lo, 0.0)
        z = z + jnp.where(a[p] + b_hi >= thr, ea[p] * eb_hi, 0.0)
    for shift in (4, 2, 1):
        z = z + pltpu.roll(z, shift, axis=0)
    inv_z = 1.0 / z
    count0, e0, rank1, e1 = [], [], [], []
    for k in range(K):
        cnt = jnp.full_like(thr, float(K))
        for r in range(K - 1, -1, -1):
            cnt = jnp.where(v0[k] + b[r] >= thr, cnt, float(r))
        count0.append(cnt)
        e0.append(jnp.exp(v0[k] - a[0]) * inv_z)
        rk = jnp.full_like(thr, float(K))
        for r in range(K - 1, -1, -1):
            rk = jnp.where(v1[k] >= b[r], float(r), rk)
        rank1.append(rk)
        e1.append(jnp.exp(v1[k] - b[0]))
    cat = lambda parts: jnp.concatenate(parts, axis=0)
    return cat(count0), cat(e0), cat(rank1), cat(e1)


def _peer_kernel(h2_ref, x1_ref, gt_ref, lg_ref, lb_ref, wq_ref, sk_ref, u0_ref, ub_ref, ua_ref, vt_ref,
                 o_ref, s_scr, cnt_scr, e0_scr, re_scr, sta_scr, stb_scr, p_scr, acc_scr, *, alpha):
    e = pl.program_id(1)
    tt = h2_ref.shape[1]
    NK, H = PEER_NKEYS, PEER_HEADS
    half = ub_ref.shape[0]
    eb = 2 * half
    LC = tt // LANES
    BT = 16

    @pl.when(e == 0)
    def _prologue():
        qt = jnp.dot(wq_ref[...], h2_ref[...], preferred_element_type=F32).astype(BF16)
        for m in range(2 * H):
            sm = jnp.dot(sk_ref[m], qt[m * PEER_HALF:(m + 1) * PEER_HALF, :],
                         preferred_element_type=F32)
            for lc in range(LC):
                s_scr[lc, m * NK:(m + 1) * NK, :] = sm[:, lc * LANES:(lc + 1) * LANES]

        def chunk(lc, carry):
            for h in range(H):
                c0, e0, r1, e1 = _peer_select(s_scr[lc, (2 * h) * NK:(2 * h + 1) * NK, :],
                                              s_scr[lc, (2 * h + 1) * NK:(2 * h + 2) * NK, :])
                cnt_scr[lc, h * NK:(h + 1) * NK, :] = c0
                e0_scr[lc, h * NK:(h + 1) * NK, :] = e0
                t0 = h * (NK // BT)
                re_scr[lc, t0:t0 + NK // BT, 0] = r1.astype(BF16).reshape(NK // BT, BT, LANES)
                re_scr[lc, t0:t0 + NK // BT, 1] = e1.astype(BF16).reshape(NK // BT, BT, LANES)
            return carry

        lax.fori_loop(0, LC, chunk, 0)
        acc_scr[...] = jnp.zeros_like(acc_scr)
        sta_scr[...] = jnp.dot(u0_ref[...], h2_ref[...], preferred_element_type=F32)

    def gate_block(ii, st_scr, r0):
        i = e * (eb // NK) + ii
        for lc in range(LC):
            w = jnp.zeros((NK // BT, BT, LANES), BF16)
            for h in range(H):
                cnt = cnt_scr[lc, pl.ds(h * NK + i, 1), :]
                e0 = e0_scr[lc, pl.ds(h * NK + i, 1), :]
                cnt = jnp.broadcast_to(cnt, (BT, LANES)).astype(BF16)
                e0 = jnp.broadcast_to(e0, (BT, LANES)).astype(BF16)
                re = re_scr[lc, h * (NK // BT):(h + 1) * (NK // BT)]
                rk, e1 = re[:, 0], re[:, 1]
                w = w + jnp.where(rk < cnt[None], e1, jnp.zeros_like(e1)) * e0[None]
            sv = st_scr[r0:r0 + NK, lc * LANES:(lc + 1) * LANES]
            g = (0.5 * sv * (1.0 + lax.erf(sv * (2.0 ** -0.5)))).astype(BF16)
            p_scr[ii * NK:(ii + 1) * NK, lc * LANES:(lc + 1) * LANES] = w.reshape(NK, LANES) * g

    def second(c0, n):
        return jnp.dot(vt_ref[:, c0:c0 + n], p_scr[c0:c0 + n, :], preferred_element_type=F32)

    h2 = h2_ref[...]
    nblk = half // NK
    gate_block(0, sta_scr, 0)
    stb_scr[...] = jnp.dot(ub_ref[...], h2, preferred_element_type=F32)
    out = None
    for k in range(1, nblk):
        gate_block(k, sta_scr, k * NK)
        if k % 2 == 1:
            part = second((k - 1) * NK, 2 * NK)
            out = part if out is None else out + part
    gate_block(nblk, stb_scr, 0)
    sta_scr[...] = jnp.dot(ua_ref[...], h2, preferred_element_type=F32)
    for k in range(1, nblk):
        gate_block(nblk + k, stb_scr, k * NK)
        if k % 2 == 1:
            out = out + second((nblk + k - 1) * NK, 2 * NK)
    acc_scr[...] += out

    @pl.when(e == pl.num_programs(1) - 1)
    def _epilogue():
        r = alpha * x1_ref[...] + _rep(_col(gt_ref[...]), tt) * acc_scr[...]
        mu = jnp.mean(r, axis=0, keepdims=True)
        dlt = r - mu
        var = jnp.mean(dlt * dlt, axis=0, keepdims=True)
        y = dlt * lax.rsqrt(var + LN_EPS) * _rep(_col(lg_ref[...]), tt) + _rep(_col(lb_ref[...]), tt)
        o_ref[...] = y.T


def _peer(h2_t, x1_t, gt2, ln_g, ln_b, wq_t, sk, u, v_t, S, alpha):
    D, T = h2_t.shape
    tt = 512
    eb = 2048
    per_b = S // tt
    half = eb // 2
    n_half = u.shape[0] // half
    n_exp = u.shape[0]
    H, NK = PEER_HEADS, PEER_NKEYS
    shared = pl.BlockSpec((1, D), lambda t, e: (0, 0))
    return pl.pallas_call(
        functools.partial(_peer_kernel, alpha=alpha),
        out_shape=jax.ShapeDtypeStruct((T, D), F32),
        grid=(T // tt, n_exp // eb),
        in_specs=[pl.BlockSpec((D, tt), lambda t, e: (0, t)),
                  pl.BlockSpec((D, tt), lambda t, e: (0, t)),
                  pl.BlockSpec((None, 1, D), lambda t, e: (t // per_b, 0, 0)),
                  shared, shared,
                  pl.BlockSpec(wq_t.shape, lambda t, e: (0, 0)),
                  pl.BlockSpec(sk.shape, lambda t, e: (0, 0, 0)),
                  pl.BlockSpec((half, D), lambda t, e: (0, 0)),
                  pl.BlockSpec((half, D), lambda t, e: (2 * e + 1, 0)),
                  pl.BlockSpec((half, D), lambda t, e: (jnp.minimum(2 * e + 2, n_half - 1), 0)),
                  pl.BlockSpec((D, eb), lambda t, e: (0, e))],
        out_specs=pl.BlockSpec((tt, D), lambda t, e: (t, 0)),
        scratch_shapes=[pltpu.VMEM((tt // LANES, 2 * H * NK, LANES), F32),
                        pltpu.VMEM((tt // LANES, H * NK, LANES), F32),
                        pltpu.VMEM((tt // LANES, H * NK, LANES), F32),
                        pltpu.VMEM((tt // LANES, H * NK // 16, 2, 16, LANES), BF16),
                        pltpu.VMEM((half, tt), F32),
                        pltpu.VMEM((half, tt), F32),
                        pltpu.VMEM((eb, tt), BF16),
                        pltpu.VMEM((D, tt), F32)],
        compiler_params=_params(("parallel", "arbitrary")),
        name="peer",
    )(h2_t, x1_t, gt2, ln_g, ln_b, wq_t, sk, u, u, u, v_t)


def kernel(x, c, w_ada, b_ada, w_in, w_branch_a, w_branch_b, w_out, hgrn_lb, hgrn_norm_g,
           ln1_g, ln1_b, peer_wq, peer_subkeys, peer_u, peer_v, ln2_g, ln2_b):
    B, S, D = x.shape
    depth = w_in.shape[0]
    assert depth == 1 and S % 512 == 0 and S <= 4096
    alpha = (2.0 * depth) ** 0.25
    T = B * S
    mw = MOBA_HEADS * MOBA_HEAD_DIM
    hw = HGRN_HEADS * HGRN_DIM

    ada = _ada(c, w_ada[0], b_ada[0])
    sh1, sc1, gt1, sh2, sc2, gt2 = [ada[:, i * D:(i + 1) * D].reshape(B, 1, D) for i in range(6)]

    w = w_in[0]
    o_v, o_h, o_g = 2 * mw, 3 * mw, 3 * mw + 4 * hw
    w_std = jnp.concatenate([w[:, :o_v], w[:, o_h:o_g]], axis=1).astype(BF16)
    w_tr = jnp.concatenate([w[:, o_g:], w[:, o_v:o_h]], axis=1).T.astype(BF16)
    x2 = x.reshape(T, D)
    proj_std, proj_t = _in_proj(x2, sc1, sh1, w_std, w_tr, S)

    ya_t = _moba(proj_std, proj_t, B, S, q_blk=0, k_blk=mw // LANES, v_blk=2 * D // LANES)
    yb = _hgrn(proj_std, hgrn_lb, hgrn_norm_g, B, S, q_blk=2 * mw // LANES)

    x1_t, h2_t = _merge(x2, ya_t, yb, proj_t, gt1, sc2, sh2, ln1_g, ln1_b,
                        w_branch_a[0].T.astype(BF16), w_branch_b[0].T.astype(BF16),
                        w_out[0].T.astype(BF16), S, alpha)

    sk = peer_subkeys[0].reshape(2 * PEER_HEADS, PEER_NKEYS, PEER_HALF).astype(BF16)
    out = _peer(h2_t, x1_t, gt2, ln2_g, ln2_b, peer_wq[0].T.astype(BF16), sk,
                peer_u[0].astype(BF16), peer_v[0].T.astype(BF16), S, alpha)
    return out.reshape(B, S, D)
```

```python
import functools

import jax
import jax.numpy as jnp
from jax import lax
from jax.experimental import pallas as pl
from jax.experimental.pallas import tpu as pltpu

F32 = jnp.float32
BF16 = jnp.bfloat16

LANES = 128
VMEM_LIMIT = 56 * 1024 * 1024

MOBA_HEADS = 8
MOBA_HEAD_DIM = 64
MOBA_BLOCK = 256
MOBA_TOPK = 3
HGRN_HEADS = 4
HGRN_DIM = 128
HGRN_CHUNK = 64
PEER_HEADS = 8
PEER_NKEYS = 128
PEER_HALF = 64
PEER_TOPK = 16
LN_EPS = 1e-5
RMS_EPS = 1e-6
NEG = -1e30

NT_DIMS = (((1,), (1,)), ((), ()))


def _params(sem):
    return pltpu.CompilerParams(dimension_semantics=sem, vmem_limit_bytes=VMEM_LIMIT)


def _silu(v):
    return v * jax.nn.sigmoid(v)


def _col(row):
    return jnp.broadcast_to(row, (LANES, row.shape[1])).T


def _rep(col, n):
    return jnp.concatenate([col] * (n // LANES), axis=1)


def _ada_kernel(c_ref, w_ref, b_ref, o_ref):
    sc = _silu(c_ref[...])
    o_ref[...] = jnp.dot(sc, w_ref[...], precision=lax.Precision.HIGHEST,
                         preferred_element_type=F32) + b_ref[...]


def _ada(c, w, b):
    B, D = c.shape
    N = w.shape[1]
    tn = 1536
    return pl.pallas_call(
        _ada_kernel,
        out_shape=jax.ShapeDtypeStruct((B, N), F32),
        grid=(N // tn,),
        in_specs=[pl.BlockSpec((B, D), lambda j: (0, 0)),
                  pl.BlockSpec((D, tn), lambda j: (0, j)),
                  pl.BlockSpec((1, tn), lambda j: (0, j))],
        out_specs=pl.BlockSpec((B, tn), lambda j: (0, j)),
        compiler_params=_params(("arbitrary",)),
        name="ada",
    )(c, w, b.reshape(1, N))


def _in_proj_kernel(x_ref, sc_ref, sh_ref, ws_ref, wt_ref, os_ref, ot_ref):
    h = (x_ref[...] * (1.0 + sc_ref[...]) + sh_ref[...]).astype(BF16)
    cw = 512
    for j in range(ws_ref.shape[1] // cw):
        os_ref[:, j * cw:(j + 1) * cw] = jnp.dot(
            h, ws_ref[:, j * cw:(j + 1) * cw], preferred_element_type=F32).astype(BF16)
    for j in range(wt_ref.shape[0] // cw):
        ot_ref[j * cw:(j + 1) * cw, :] = lax.dot_general(
            wt_ref[j * cw:(j + 1) * cw, :], h, NT_DIMS, preferred_element_type=F32).astype(BF16)


def _in_proj(x2, sc, sh, w_std, w_tr, S):
    T, D = x2.shape
    ns, nt = w_std.shape[1], w_tr.shape[0]
    tm = 512
    per_b = S // tm
    return pl.pallas_call(
        _in_proj_kernel,
        out_shape=(jax.ShapeDtypeStruct((T, ns), BF16), jax.ShapeDtypeStruct((nt, T), BF16)),
        grid=(T // tm,),
        in_specs=[pl.BlockSpec((tm, D), lambda i: (i, 0)),
                  pl.BlockSpec((None, 1, D), lambda i: (i // per_b, 0, 0)),
                  pl.BlockSpec((None, 1, D), lambda i: (i // per_b, 0, 0)),
                  pl.BlockSpec((D, ns), lambda i: (0, 0)),
                  pl.BlockSpec((nt, D), lambda i: (0, 0))],
        out_specs=(pl.BlockSpec((tm, ns), lambda i: (i, 0)),
                   pl.BlockSpec((nt, tm), lambda i: (0, i))),
        compiler_params=_params(("parallel",)),
        name="in_proj",
    )(x2, sc, sh, w_std, w_tr)


def _moba_kernel(q_ref, k_ref, vt_ref, o_ref):
    S = q_ref.shape[0]
    BS, dh = MOBA_BLOCK, MOBA_HEAD_DIM
    nb = S // BS
    hp = pl.program_id(1)
    q2 = q_ref[...]
    k2 = k_ref[...]
    kmean = jnp.mean(k2.astype(F32).reshape(nb, BS, LANES), axis=1)
    lane = lax.broadcasted_iota(jnp.int32, (1, LANES), 1)
    krow = lax.broadcasted_iota(jnp.int32, (S, LANES), 0)
    pos_hi = (krow >> 4).astype(F32)
    pos_lo = (krow & 15).astype(F32)
    nidx = lax.broadcasted_iota(jnp.int32, (nb, S), 0)
    qblk = lax.broadcasted_iota(jnp.int32, (nb, S), 1) // BS
    past = nidx < qblk
    causal = (lax.broadcasted_iota(jnp.int32, (BS, BS), 0)
              <= lax.broadcasted_iota(jnp.int32, (BS, BS), 1))

    for hh in range(2):
        head = (hp * 2 + hh).astype(F32)
        slope = jnp.exp2(jnp.full((1, LANES), -8.0 / MOBA_HEADS, F32) * (head + 1.0))
        own = (lane >= hh * dh) & (lane < (hh + 1) * dh)
        f0 = (1 - hh) * dh
        kh = jnp.where(own, k2.astype(F32),
                       jnp.where(lane == f0, pos_hi, jnp.where(lane == f0 + 1, pos_lo, 0.0))
                       ).astype(BF16)
        qh = jnp.where(own, q2.astype(F32) * (dh ** -0.5),
                       jnp.where(lane == f0, 16.0 * slope, jnp.where(lane == f0 + 1, slope, 0.0))
                       ).astype(BF16)
        kmh = jnp.where(own, kmean, 0.0).astype(BF16)
        gate = lax.dot_general(kmh, qh, NT_DIMS, preferred_element_type=F32)
        gate = jnp.where(past, gate, -jnp.inf)
        rank = jnp.zeros((nb, S), F32)
        for n2 in range(nb):
            row = gate[n2:n2 + 1, :]
            beats = (row > gate) | ((row == gate) & (n2 < nidx))
            rank = rank + jnp.where(beats, 1.0, 0.0)
        sel = past & (rank < float(MOBA_TOPK))

        for j in range(nb):
            qj = qh[j * BS:(j + 1) * BS, :]
            st = lax.dot_general(kh[:(j + 1) * BS, :], qj, NT_DIMS,
                                 preferred_element_type=F32)
            blocks = []
            for n in range(j):
                keep = sel[n:n + 1, j * BS:(j + 1) * BS]
                blocks.append(jnp.where(keep, st[n * BS:(n + 1) * BS, :], NEG))
            blocks.append(jnp.where(causal, st[j * BS:(j + 1) * BS, :], NEG))
            m = blocks[0].max(axis=0, keepdims=True)
            for blk in blocks[1:]:
                m = jnp.maximum(m, blk.max(axis=0, keepdims=True))
            l = jnp.zeros((1, BS), F32)
            probs = []
            for blk in blocks:
                p = jnp.exp(blk - m)
                l = l + p.sum(axis=0, keepdims=True)
                probs.append(p.astype(BF16))
            acc = jnp.dot(vt_ref[:, :(j + 1) * BS], jnp.concatenate(probs, axis=0),
                          preferred_element_type=F32)
            o = acc[hh * dh:(hh + 1) * dh, :] * (1.0 / l)
            o_ref[hh * dh:(hh + 1) * dh, j * BS:(j + 1) * BS] = o.astype(BF16)


def _moba(proj_std, proj_t, B, S, q_blk, k_blk, v_blk):
    T = B * S
    width = MOBA_HEADS * MOBA_HEAD_DIM
    npair = width // LANES
    return pl.pallas_call(
        _moba_kernel,
        out_shape=jax.ShapeDtypeStruct((width, T), BF16),
        grid=(B, npair),
        in_specs=[pl.BlockSpec((S, LANES), lambda b, h: (b, q_blk + h)),
                  pl.BlockSpec((S, LANES), lambda b, h: (b, k_blk + h)),
                  pl.BlockSpec((LANES, S), lambda b, h: (v_blk + h, b))],
        out_specs=pl.BlockSpec((LANES, S), lambda b, h: (h, b)),
        compiler_params=_params(("parallel", "parallel")),
        name="moba",
    )(proj_std, proj_std, proj_t)


def _hgrn_kernel(q_ref, f_ref, i_ref, g_ref, lb_ref, ng_ref, o_ref, ds_scr, st_scr):
    S, d = q_ref.shape
    C = HGRN_CHUNK
    nC = S // C
    lbl = lb_ref[...]
    e = jnp.exp(lbl - lbl.max(axis=0, keepdims=True))
    lb = e[0:1, :] / e.sum(axis=0, keepdims=True)
    f = lb + (1.0 - lb) * jax.nn.sigmoid(f_ref[...].astype(F32))
    lf = jnp.log(f)
    pos = lax.broadcasted_iota(jnp.int32, (S, d), 0) & (C - 1)
    b = lf
    sh = 1
    while sh < C:
        b = b + jnp.where(pos >= sh, pltpu.roll(b, sh, axis=0), 0.0)
        sh *= 2
    b3 = b.reshape(nC, C, d)
    bref = b3[:, C // 2:C // 2 + 1, :]
    blast = b3[:, C - 1:C, :]
    q3 = _silu(q_ref[...].astype(F32)).reshape(nC, C, d)
    k3 = (1.0 - f).reshape(nC, C, d)
    v3 = i_ref[...].reshape(nC, C, d)
    qe = (q3 * jnp.exp(b3 - bref)).astype(BF16)
    ke = (k3 * jnp.exp(bref - b3)).astype(BF16)
    a = jnp.einsum('ctd,csd->cts', qe, ke, preferred_element_type=F32)
    tril = (lax.broadcasted_iota(jnp.int32, (C, C), 0) >= lax.broadcasted_iota(jnp.int32, (C, C), 1))
    a = jnp.where(tril[None], a, 0.0).astype(BF16)
    o_intra = jnp.einsum('cts,csv->ctv', a, v3, preferred_element_type=F32)
    kd = (k3 * jnp.exp(blast - b3)).astype(BF16)
    v3t = jnp.swapaxes(v3.astype(F32), 1, 2).astype(BF16)
    ds_scr[...] = jnp.einsum('cvs,csd->cvd', v3t, kd, preferred_element_type=F32)
    decay = jnp.exp(blast)

    st = jnp.zeros((d, d), F32)
    for c in range(nC):
        st_scr[c] = st
        st = st * decay[c] + ds_scr[c]
    qb = (q3 * jnp.exp(b3)).astype(BF16)
    o_inter = jnp.einsum('ctd,cvd->ctv', qb, st_scr[...].astype(BF16), preferred_element_type=F32)
    o = o_intra + o_inter
    o = o * lax.rsqrt(jnp.mean(o * o, axis=-1, keepdims=True) + RMS_EPS)
    o = o.reshape(S, d) * ng_ref[...] * _silu(g_ref[...].astype(F32))
    o_ref[...] = o.astype(BF16)


def _hgrn(proj_std, lb, norm_g, B, S, q_blk):
    T = B * S
    H, d = HGRN_HEADS, HGRN_DIM
    nC = S // HGRN_CHUNK

    def col(off):
        return pl.BlockSpec((S, d), lambda b, h: (b, q_blk + off * H + h))

    return pl.pallas_call(
        _hgrn_kernel,
        out_shape=jax.ShapeDtypeStruct((T, H * d), BF16),
        grid=(B, H),
        in_specs=[col(0), col(1), col(2), col(3),
                  pl.BlockSpec((lb.shape[0], d), lambda b, h: (0, h)),
                  pl.BlockSpec((1, d), lambda b, h: (0, h))],
        out_specs=pl.BlockSpec((S, d), lambda b, h: (b, h)),
        scratch_shapes=[pltpu.VMEM((nC, d, d), F32), pltpu.VMEM((nC, d, d), F32)],
        compiler_params=_params(("parallel", "parallel")),
        name="hgrn",
    )(proj_std, proj_std, proj_std, proj_std, lb, norm_g)


def _merge_kernel(x_ref, yat_ref, yb_ref, ga_ref, gb_ref, gt_ref, sc_ref, sh_ref, lg_ref, lb_ref,
                  wa_ref, wb_ref, wo_ref, x1_ref, h2_ref, *, alpha):
    tm = x_ref.shape[0]
    at = jnp.dot(wa_ref[...], yat_ref[...], preferred_element_type=F32)
    bt = lax.dot_general(wb_ref[...], yb_ref[...], NT_DIMS, preferred_element_type=F32)
    mixed = (jax.nn.sigmoid(ga_ref[...].astype(F32)) * at
             + jax.nn.sigmoid(gb_ref[...].astype(F32)) * bt).astype(BF16)
    ot = jnp.dot(wo_ref[...], mixed, preferred_element_type=F32)
    r = alpha * x_ref[...].T + _rep(_col(gt_ref[...]), tm) * ot
    mu = jnp.mean(r, axis=0, keepdims=True)
    dlt = r - mu
    var = jnp.mean(dlt * dlt, axis=0, keepdims=True)
    x1 = dlt * lax.rsqrt(var + LN_EPS) * _rep(_col(lg_ref[...]), tm) + _rep(_col(lb_ref[...]), tm)
    x1_ref[...] = x1
    h2 = x1 * (1.0 + _rep(_col(sc_ref[...]), tm)) + _rep(_col(sh_ref[...]), tm)
    h2_ref[...] = h2.astype(BF16)


def _merge(x2, ya_t, yb, proj_t, gt1, sc2, sh2, ln_g, ln_b, wa_t, wb_t, wo_t, S, alpha):
    T, D = x2.shape
    tm = 512
    per_b = S // tm
    wa_w, wb_w = wa_t.shape[1], wb_t.shape[1]
    row = pl.BlockSpec((None, 1, D), lambda i: (i // per_b, 0, 0))
    shared = pl.BlockSpec((1, D), lambda i: (0, 0))
    return pl.pallas_call(
        functools.partial(_merge_kernel, alpha=alpha),
        out_shape=(jax.ShapeDtypeStruct((D, T), F32), jax.ShapeDtypeStruct((D, T), BF16)),
        grid=(T // tm,),
        in_specs=[pl.BlockSpec((tm, D), lambda i: (i, 0)),
                  pl.BlockSpec((wa_w, tm), lambda i: (0, i)),
                  pl.BlockSpec((tm, wb_w), lambda i: (i, 0)),
                  pl.BlockSpec((D, tm), lambda i: (0, i)),
                  pl.BlockSpec((D, tm), lambda i: (1, i)),
                  row, row, row, shared, shared,
                  pl.BlockSpec((D, wa_w), lambda i: (0, 0)),
                  pl.BlockSpec((D, wb_w), lambda i: (0, 0)),
                  pl.BlockSpec((D, D), lambda i: (0, 0))],
        out_specs=(pl.BlockSpec((D, tm), lambda i: (0, i)),
                   pl.BlockSpec((D, tm), lambda i: (0, i))),
        compiler_params=_params(("parallel",)),
        name="merge",
    )(x2, ya_t, yb, proj_t, proj_t, gt1, sc2, sh2, ln_g, ln_b, wa_t, wb_t, wo_t)


def _sort_network(n):
    pairs, p = [], 1
    while p < n:
        k = p
        while k >= 1:
            for j in range(k % p, n - k, 2 * k):
                for i in range(min(k, n - j - k)):
                    if (i + j) // (2 * p) == (i + j + k) // (2 * p):
                        pairs.append((i + j, i + j + k))
            k //= 2
        p *= 2
    return pairs


def _exchange(v, i, j):
    v[i], v[j] = jnp.maximum(v[i], v[j]), jnp.minimum(v[i], v[j])


def _top16(v, sort_last=True):
    K = PEER_TOPK
    v = list(v)
    for i, j in _sort_network(len(v)):
        _exchange(v, i, j)
    v = v + [jnp.full_like(v[0], -jnp.inf)] * (K - len(v))
    for shift in (1, 2, 4):
        w = [pltpu.roll(x, shift, axis=0) for x in v]
        v = [jnp.maximum(v[k], w[K - 1 - k]) for k in range(K)]
        if shift < 4 or sort_last:
            d = K // 2
            while d >= 1:
                for k in range(K):
                    if k & d == 0:
                        _exchange(v, k, k + d)
                d //= 2
    return v


def _peer_select(s0, s1):
    K = PEER_TOPK
    v0 = [s0[8 * k:8 * k + 8, :] for k in range(K)]
    v1 = [s1[8 * k:8 * k + 8, :] for k in range(K)]
    a = _top16(v0)
    b = _top16(v1)
    sub = lax.broadcasted_iota(jnp.int32, v0[0].shape, 0)

    def pack(vals):
        out = vals[7]
        for r in range(6, -1, -1):
            out = jnp.where(sub == r, vals[r], out)
        return out

    b_lo, b_hi, a_hi = pack(b[0:8]), pack(b[8:16]), pack(a[8:16])
    cands = [a[0] + b_lo, a[0] + b_hi, a[1] + b_lo]
    for p in range(2, 8):
        cands.append(jnp.where(sub < K // (p + 1), a[p] + b_lo, -jnp.inf))
    cands.append(a_hi + b[0])
    top = _top16(cands, sort_last=False)
    thr = top[0]
    for t in top[1:]:
        thr = jnp.minimum(thr, t)
    ea = [jnp.exp(x - a[0]) for x in a]
    eb = [jnp.exp(x - b[0]) for x in b]
    eb_lo, eb_hi = pack(eb[0:8]), pack(eb[8:16])
    z = jnp.zeros_like(thr)
    for p in range(K):
        z = z + jnp.where(a[p] + b_lo >= thr, ea[p] * eb_lo, 0.0)
        z = z + jnp.where(a[p] + b_hi >= thr, ea[p] * eb_hi, 0.0)
    for shift in (4, 2, 1):
        z = z + pltpu.roll(z, shift, axis=0)
    inv_z = 0.5 / z
    count0, e0, rank1, e1 = [], [], [], []
    for k in range(K):
        cnt = jnp.full_like(thr, float(K))
        for r in range(K - 1, -1, -1):
            cnt = jnp.where(v0[k] + b[r] >= thr, cnt, float(r))
        count0.append(cnt)
        e0.append(jnp.exp(v0[k] - a[0]) * inv_z)
        rk = jnp.full_like(thr, float(K))
        for r in range(K - 1, -1, -1):
            rk = jnp.where(v1[k] >= b[r], float(r), rk)
        rank1.append(rk)
        e1.append(jnp.exp(v1[k] - b[0]))
    cat = lambda parts: jnp.concatenate(parts, axis=0)
    return cat(count0), cat(e0), cat(rank1), cat(e1)


def _peer_kernel(h2_ref, x1_ref, gt_ref, lg_ref, lb_ref, wq_ref, sk_ref, u0_ref, ub_ref, ua_ref, vt_ref,
                 o_ref, s_scr, cnt_scr, e0_scr, re_scr, sta_scr, stb_scr, p_scr, acc_scr, *, alpha):
    e = pl.program_id(1)
    tt = h2_ref.shape[1]
    NK, H = PEER_NKEYS, PEER_HEADS
    half = ub_ref.shape[0]
    eb = 2 * half
    LC = tt // LANES
    BT = 16

    @pl.when(e == 0)
    def _prologue():
        qt = jnp.dot(wq_ref[...], h2_ref[...], preferred_element_type=F32).astype(BF16)
        for m in range(2 * H):
            sm = jnp.dot(sk_ref[m], qt[m * PEER_HALF:(m + 1) * PEER_HALF, :],
                         preferred_element_type=F32)
            for lc in range(LC):
                s_scr[lc, m * NK:(m + 1) * NK, :] = sm[:, lc * LANES:(lc + 1) * LANES]

        def chunk(lc, carry):
            for h in range(H):
                c0, e0, r1, e1 = _peer_select(s_scr[lc, (2 * h) * NK:(2 * h + 1) * NK, :],
                                              s_scr[lc, (2 * h + 1) * NK:(2 * h + 2) * NK, :])
                cnt_scr[lc, h * NK:(h + 1) * NK, :] = c0
                e0_scr[lc, h * NK:(h + 1) * NK, :] = e0
                t0 = h * (NK // BT)
                re_scr[lc, t0:t0 + NK // BT, 0] = r1.astype(BF16).reshape(NK // BT, BT, LANES)
                re_scr[lc, t0:t0 + NK // BT, 1] = e1.astype(BF16).reshape(NK // BT, BT, LANES)
            return carry

        lax.fori_loop(0, LC, chunk, 0)
        acc_scr[...] = jnp.zeros_like(acc_scr)
        sta_scr[...] = jnp.dot(u0_ref[...], h2_ref[...], preferred_element_type=F32)

    def gate_block(ii, st_scr, r0):
        i = e * (eb // NK) + ii
        for lc in range(LC):
            w = jnp.zeros((NK // BT, BT, LANES), BF16)
            for h in range(H):
                cnt = cnt_scr[lc, pl.ds(h * NK + i, 1), :]
                e0 = e0_scr[lc, pl.ds(h * NK + i, 1), :]
                cnt = jnp.broadcast_to(cnt, (BT, LANES)).astype(BF16)
                e0 = jnp.broadcast_to(e0, (BT, LANES)).astype(BF16)
                re = re_scr[lc, h * (NK // BT):(h + 1) * (NK // BT)]
                rk, e1 = re[:, 0], re[:, 1]
                w = w + jnp.where(rk < cnt[None], e1, jnp.zeros_like(e1)) * e0[None]
            sv = st_scr[r0:r0 + NK, lc * LANES:(lc + 1) * LANES]
            g = (sv * (1.0 + lax.erf(sv * (2.0 ** -0.5)))).astype(BF16)
            p_scr[ii * NK:(ii + 1) * NK, lc * LANES:(lc + 1) * LANES] = w.reshape(NK, LANES) * g

    def second(c0, n):
        return jnp.dot(vt_ref[:, c0:c0 + n], p_scr[c0:c0 + n, :], preferred_element_type=F32)

    h2 = h2_ref[...]
    nblk = half // NK
    gate_block(0, sta_scr, 0)
    stb_scr[...] = jnp.dot(ub_ref[...], h2, preferred_element_type=F32)
    out = None
    for k in range(1, nblk):
        gate_block(k, sta_scr, k * NK)
        if k % 2 == 1:
            part = second((k - 1) * NK, 2 * NK)
            out = part if out is None else out + part
    gate_block(nblk, stb_scr, 0)
    sta_scr[...] = jnp.dot(ua_ref[...], h2, preferred_element_type=F32)
    for k in range(1, nblk):
        gate_block(nblk + k, stb_scr, k * NK)
        if k % 2 == 1:
            out = out + second((nblk + k - 1) * NK, 2 * NK)
    acc_scr[...] += out

    @pl.when(e == pl.num_programs(1) - 1)
    def _epilogue():
        r = alpha * x1_ref[...] + _rep(_col(gt_ref[...]), tt) * acc_scr[...]
        mu = jnp.mean(r, axis=0, keepdims=True)
        dlt = r - mu
        var = jnp.mean(dlt * dlt, axis=0, keepdims=True)
        y = dlt * lax.rsqrt(var + LN_EPS) * _rep(_col(lg_ref[...]), tt) + _rep(_col(lb_ref[...]), tt)
        o_ref[...] = y.T


def _peer(h2_t, x1_t, gt2, ln_g, ln_b, wq_t, sk, u, v_t, S, alpha):
    D, T = h2_t.shape
    tt = 512
    eb = 2048
    per_b = S // tt
    half = eb // 2
    n_half = u.shape[0] // half
    n_exp = u.shape[0]
    H, NK = PEER_HEADS, PEER_NKEYS
    shared = pl.BlockSpec((1, D), lambda t, e: (0, 0))
    return pl.pallas_call(
        functools.partial(_peer_kernel, alpha=alpha),
        out_shape=jax.ShapeDtypeStruct((T, D), F32),
        grid=(T // tt, n_exp // eb),
        in_specs=[pl.BlockSpec((D, tt), lambda t, e: (0, t)),
                  pl.BlockSpec((D, tt), lambda t, e: (0, t)),
                  pl.BlockSpec((None, 1, D), lambda t, e: (t // per_b, 0, 0)),
                  shared, shared,
                  pl.BlockSpec(wq_t.shape, lambda t, e: (0, 0)),
                  pl.BlockSpec(sk.shape, lambda t, e: (0, 0, 0)),
                  pl.BlockSpec((half, D), lambda t, e: (0, 0)),
                  pl.BlockSpec((half, D), lambda t, e: (2 * e + 1, 0)),
                  pl.BlockSpec((half, D), lambda t, e: (jnp.minimum(2 * e + 2, n_half - 1), 0)),
                  pl.BlockSpec((D, eb), lambda t, e: (0, e))],
        out_specs=pl.BlockSpec((tt, D), lambda t, e: (t, 0)),
        scratch_shapes=[pltpu.VMEM((tt // LANES, 2 * H * NK, LANES), F32),
                        pltpu.VMEM((tt // LANES, H * NK, LANES), F32),
                        pltpu.VMEM((tt // LANES, H * NK, LANES), F32),
                        pltpu.VMEM((tt // LANES, H * NK // 16, 2, 16, LANES), BF16),
                        pltpu.VMEM((half, tt), F32),
                        pltpu.VMEM((half, tt), F32),
                        pltpu.VMEM((eb, tt), BF16),
                        pltpu.VMEM((D, tt), F32)],
        compiler_params=_params(("parallel", "arbitrary")),
        name="peer",
    )(h2_t, x1_t, gt2, ln_g, ln_b, wq_t, sk, u, u, u, v_t)


def kernel(x, c, w_ada, b_ada, w_in, w_branch_a, w_branch_b, w_out, hgrn_lb, hgrn_norm_g,
           ln1_g, ln1_b, peer_wq, peer_subkeys, peer_u, peer_v, ln2_g, ln2_b):
    B, S, D = x.shape
    depth = w_in.shape[0]
    assert depth == 1 and S % 512 == 0 and S <= 4096
    alpha = (2.0 * depth) ** 0.25
    T = B * S
    mw = MOBA_HEADS * MOBA_HEAD_DIM
    hw = HGRN_HEADS * HGRN_DIM

    ada = _ada(c, w_ada[0], b_ada[0])
    sh1, sc1, gt1, sh2, sc2, gt2 = [ada[:, i * D:(i + 1) * D].reshape(B, 1, D) for i in range(6)]

    w = w_in[0]
    o_v, o_h, o_g = 2 * mw, 3 * mw, 3 * mw + 4 * hw
    w_std = jnp.concatenate([w[:, :o_v], w[:, o_h:o_g]], axis=1).astype(BF16)
    w_tr = jnp.concatenate([w[:, o_g:], w[:, o_v:o_h]], axis=1).T.astype(BF16)
    x2 = x.reshape(T, D)
    proj_std, proj_t = _in_proj(x2, sc1, sh1, w_std, w_tr, S)

    ya_t = _moba(proj_std, proj_t, B, S, q_blk=0, k_blk=mw // LANES, v_blk=2 * D // LANES)
    yb = _hgrn(proj_std, hgrn_lb, hgrn_norm_g, B, S, q_blk=2 * mw // LANES)

    x1_t, h2_t = _merge(x2, ya_t, yb, proj_t, gt1, sc2, sh2, ln1_g, ln1_b,
                        w_branch_a[0].T.astype(BF16), w_branch_b[0].T.astype(BF16),
                        w_out[0].T.astype(BF16), S, alpha)

    sk = peer_subkeys[0].reshape(2 * PEER_HEADS, PEER_NKEYS, PEER_HALF).astype(BF16)
    out = _peer(h2_t, x1_t, gt2, ln2_g, ln2_b, peer_wq[0].T.astype(BF16), sk,
                peer_u[0].astype(BF16), peer_v[0].T.astype(BF16), S, alpha)
    return out.reshape(B, S, D)
```
